```python
import math, functools
import jax, jax.numpy as jnp
from jax import lax
import numpy as np

D_MODEL = 4096
BATCH = 2
SEQ = 4096
DEPTH = 1
DEC_BATCH = 128
DEC_SEQ = 4
PAST_LEN = 8192
PAGE_SIZE = 128

N_META = 16
WINDOW = 128
ATTN_BLOCK = WINDOW
A_HEADS = D_MODEL // 128
A_KV_HEADS = A_HEADS // 4
A_GROUP = A_HEADS // A_KV_HEADS
A_HEAD_DIM = 64
ROT_DIM = A_HEAD_DIM // 4
ROPE_THETA = 500000.0
B_HEADS = D_MODEL // 256
B_HEAD_DIM = 128
CONV_W = 4
DELTA_CHUNK = 64
D_FF = 4 * D_MODEL
EPS = 1e-6

QA_DIM = A_HEADS * A_HEAD_DIM
KVA_DIM = A_KV_HEADS * A_HEAD_DIM
QB_DIM = B_HEADS * B_HEAD_DIM
CONV_DIM = 3 * QB_DIM
IN_SIZES = (QA_DIM, KVA_DIM, KVA_DIM, CONV_DIM, QB_DIM, B_HEADS, B_HEADS, D_MODEL, D_MODEL)
IN_DIM = QA_DIM + 2 * KVA_DIM + CONV_DIM + QB_DIM + 2 * B_HEADS + 2 * D_MODEL

kernel_name = 'hybrid_swa_sink_gated_deltanet_decoder_step'


def rmsnorm(x, w):
    xf = x.astype(jnp.float32)
    y = xf * lax.rsqrt(jnp.mean(xf * xf, axis=-1, keepdims=True) + EPS)
    return (y * w.astype(jnp.float32)).astype(x.dtype)


def l2norm(x):
    xf = x.astype(jnp.float32)
    return xf * lax.rsqrt(jnp.sum(xf * xf, axis=-1, keepdims=True) + EPS)


def partial_rope(x, pos):
    half = ROT_DIM // 2
    inv_freq = ROPE_THETA ** (-jnp.arange(half, dtype=jnp.float32) * (2.0 / ROT_DIM))
    ang = pos.astype(jnp.float32)[:, None] * inv_freq[None, :]
    cos = jnp.cos(ang)[:, None, :]
    sin = jnp.sin(ang)[:, None, :]
    xr = x[..., :ROT_DIM].astype(jnp.float32)
    x1, x2 = xr[..., :half], xr[..., half:]
    rot = jnp.concatenate([x1 * cos - x2 * sin, x2 * cos + x1 * sin], axis=-1)
    return jnp.concatenate([rot.astype(x.dtype), x[..., ROT_DIM:]], axis=-1)


def sink_probs(s, mask, sinks):
    sk = sinks.astype(jnp.float32).reshape(A_KV_HEADS, A_GROUP, 1, 1)
    s = jnp.where(mask, s, -jnp.inf)
    m = jnp.maximum(jnp.max(s, axis=-1, keepdims=True), sk)
    p = jnp.exp(s - m)
    return p / (jnp.sum(p, axis=-1, keepdims=True) + jnp.exp(sk - m))


def attend_prompt(q, k, v, sinks):
    Bn, L = q.shape[:2]
    nb = -(-L // ATTN_BLOCK)
    pad = nb * ATTN_BLOCK - L
    qb = jnp.pad(q, ((0, 0), (0, pad), (0, 0), (0, 0))).reshape(Bn, nb, ATTN_BLOCK, A_KV_HEADS, A_GROUP, A_HEAD_DIM)

    def band(x):
        xp = jnp.pad(x, ((0, 0), (ATTN_BLOCK, pad), (0, 0), (0, 0))).reshape(Bn, nb + 1, ATTN_BLOCK, A_KV_HEADS, A_HEAD_DIM)
        return jnp.concatenate([xp[:, :-1], xp[:, 1:]], axis=2)

    kb, vb = band(k), band(v)
    qpos = jnp.arange(nb)[:, None] * ATTN_BLOCK + jnp.arange(ATTN_BLOCK)[None, :]
    kpos = (jnp.arange(nb)[:, None] - 1) * ATTN_BLOCK + jnp.arange(2 * ATTN_BLOCK)[None, :]
    diff = qpos[:, :, None] - kpos[:, None, :]
    mask = (diff >= 0) & (diff <= WINDOW) & (kpos[:, None, :] >= 0)
    s = jnp.einsum('bnqhgd,bnkhd->bnhgqk', qb, kb, preferred_element_type=jnp.float32) * (A_HEAD_DIM ** -0.5)
    p = sink_probs(s, mask[None, :, None, None], sinks)
    o = jnp.einsum('bnhgqk,bnkhd->bnqhgd', p.astype(v.dtype), vb)
    o = o.reshape(Bn, nb * ATTN_BLOCK, QA_DIM)[:, :L]
    return o, k[:, -WINDOW:], v[:, -WINDOW:]


def attend_sample(q, k, v, sinks, cache_k, cache_v):
    Bd, T = q.shape[:2]
    n_buf = cache_k.shape[1]
    k_all = jnp.concatenate([cache_k.astype(k.dtype), k], axis=1)
    v_all = jnp.concatenate([cache_v.astype(v.dtype), v], axis=1)
    qpos = PAST_LEN + jnp.arange(T)
    kpos = PAST_LEN - n_buf + jnp.arange(n_buf + T)
    diff = qpos[:, None] - kpos[None, :]
    mask = (diff >= 0) & (diff <= WINDOW)
    qg = q.reshape(Bd, T, A_KV_HEADS, A_GROUP, A_HEAD_DIM)
    s = jnp.einsum('bqhgd,bkhd->bhgqk', qg, k_all, preferred_element_type=jnp.float32) * (A_HEAD_DIM ** -0.5)
    p = sink_probs(s, mask, sinks)
    o = jnp.einsum('bhgqk,bkhd->bqhgd', p.astype(v.dtype), v_all).reshape(Bd, T, QA_DIM)
    return o, k_all[:, -n_buf:], v_all[:, -n_buf:]


def short_conv(x, hist, w):
    T = x.shape[1]
    xx = jnp.concatenate([hist.astype(x.dtype), x], axis=1)
    y = xx[:, 0:T].astype(jnp.float32) * w[0].astype(jnp.float32)
    for j in range(1, CONV_W):
        y = y + xx[:, j:j + T].astype(jnp.float32) * w[j].astype(jnp.float32)
    return jax.nn.silu(y), xx[:, -(CONV_W - 1):]


def gated_delta_chunked(q, k, v, g, beta, s0, chunk):
    Bn, T, H, Dv = v.shape
    n = T // chunk

    def blk(x):
        return jnp.moveaxis(x.reshape((Bn, n, chunk) + x.shape[2:]), 2, 3)

    qc, kc, vc, gc, bc = blk(q), blk(k), blk(v), blk(g), blk(beta)
    G = jnp.cumsum(gc, axis=-1)
    tri = jnp.tril(jnp.ones((chunk, chunk), bool))
    tri_strict = jnp.tril(jnp.ones((chunk, chunk), bool), -1)
    decay = jnp.exp(jnp.where(tri, G[..., :, None] - G[..., None, :], -jnp.inf))
    kbeta = kc * bc[..., None]
    M = jnp.where(tri_strict, jnp.einsum('bnhid,bnhjd->bnhij', kbeta, kc) * decay, 0.0)
    eye = jnp.eye(chunk, dtype=jnp.float32)
    rhs = jnp.concatenate([vc * bc[..., None], kbeta * jnp.exp(G)[..., None]], axis=-1)
    sol = lax.linalg.triangular_solve(eye + M, rhs, left_side=True, lower=True, unit_diagonal=True)
    U, W = sol[..., :Dv], sol[..., Dv:]
    qk = jnp.einsum('bnhid,bnhjd->bnhij', qc, kc) * decay
    q_dec = qc * jnp.exp(G)[..., None]
    k_dec = kc * jnp.exp(G[..., -1:] - G)[..., None]
    g_last = jnp.exp(G[..., -1])

    def step(S, xs):
        U_i, W_i, qk_i, qd_i, kd_i, gl_i = xs
        v_new = U_i - jnp.einsum('bhcd,bhde->bhce', W_i, S)
        o = jnp.einsum('bhcd,bhde->bhce', qd_i, S) + jnp.einsum('bhij,bhje->bhie', qk_i, v_new)
        S = S * gl_i[..., None, None] + jnp.einsum('bhcd,bhce->bhde', kd_i, v_new)
        return S, o

    xs = tuple(jnp.moveaxis(a, 1, 0) for a in (U, W, qk, q_dec, k_dec, g_last))
    S, o = lax.scan(step, s0, xs)
    o = jnp.swapaxes(jnp.moveaxis(o, 0, 1), 2, 3).reshape(Bn, T, H, Dv)
    return o, S


def _project(hn, w_in):
    z = jnp.einsum('btd,dc->btc', hn, w_in)
    cuts, acc = [], 0
    for size in IN_SIZES[:-1]:
        acc += size
        cuts.append(acc)
    return jnp.split(z, cuts, axis=-1)


def _layer(h, pos, attend, conv_hist, s0, front_pad, chunk,
           norm_mix_pre, norm_mix_post, norm_mlp_pre, norm_mlp_post, w_in, sinks, conv_w,
           a_log, dt_bias, delta_norm, w_branch_a, w_branch_b, w_out, w_up, w_down):
    Bn, T, _ = h.shape
    hn = rmsnorm(h, norm_mix_pre)
    qa, ka, va, qkv_b, zb, b_raw, a_raw, ga, gb = _project(hn, w_in)
    qa = partial_rope(qa.reshape(Bn, T, A_HEADS, A_HEAD_DIM), pos)
    ka = partial_rope(ka.reshape(Bn, T, A_KV_HEADS, A_HEAD_DIM), pos)
    va = va.reshape(Bn, T, A_KV_HEADS, A_HEAD_DIM)
    oa, new_k, new_v = attend(qa, ka, va, sinks)
    xc, new_conv = short_conv(qkv_b, conv_hist, conv_w)
    qb, kb, vb = jnp.split(xc, [QB_DIM, 2 * QB_DIM], axis=-1)
    qb = l2norm(qb.reshape(Bn, T, B_HEADS, B_HEAD_DIM)) * (B_HEAD_DIM ** -0.5)
    kb = l2norm(kb.reshape(Bn, T, B_HEADS, B_HEAD_DIM))
    vb = vb.reshape(Bn, T, B_HEADS, B_HEAD_DIM)
    g = -jnp.exp(a_log.astype(jnp.float32)) * jax.nn.softplus(a_raw.astype(jnp.float32) + dt_bias.astype(jnp.float32))
    beta = jax.nn.sigmoid(b_raw.astype(jnp.float32))
    p4 = ((0, 0), (front_pad, 0), (0, 0), (0, 0))
    p3 = ((0, 0), (front_pad, 0), (0, 0))
    ob, s_new = gated_delta_chunked(jnp.pad(qb, p4), jnp.pad(kb, p4), jnp.pad(vb, p4),
                                    jnp.pad(g, p3), jnp.pad(beta, p3), s0.astype(jnp.float32), chunk)
    ob = ob[:, front_pad:]
    ob = rmsnorm(ob, delta_norm) * jax.nn.silu(zb.reshape(Bn, T, B_HEADS, B_HEAD_DIM).astype(jnp.float32))
    ob = ob.reshape(Bn, T, QB_DIM).astype(h.dtype)
    merged = jax.nn.sigmoid(ga) * (oa @ w_branch_a) + jax.nn.sigmoid(gb) * (ob @ w_branch_b)
    h = h + rmsnorm(merged @ w_out, norm_mix_post)
    u = jnp.square(jax.nn.relu(rmsnorm(h, norm_mlp_pre) @ w_up))
    h = h + rmsnorm(u @ w_down, norm_mlp_post)
    return h, new_k, new_v, new_conv, s_new.astype(s0.dtype)


def setup_inputs(seed: int = 0) -> dict:
    key = jax.random.key(seed)
    ks = jax.random.split(key, 24)
    f32 = jnp.float32

    def nrm(k, shape, scale):
        return jax.random.normal(k, shape, f32) * scale

    n_buf = min(WINDOW, PAST_LEN)
    dt = jnp.exp(jax.random.uniform(ks[20], (DEPTH, B_HEADS), f32, math.log(1e-3), math.log(1e-1)))
    return {
        'x_prompt': nrm(ks[0], (BATCH, SEQ, D_MODEL), 1.0),
        'x_sample': nrm(ks[1], (DEC_BATCH, DEC_SEQ, D_MODEL), 1.0),
        'cache_win_k': nrm(ks[2], (DEPTH, DEC_BATCH, n_buf, A_KV_HEADS, A_HEAD_DIM), 1.0),
        'cache_win_v': nrm(ks[3], (DEPTH, DEC_BATCH, n_buf, A_KV_HEADS, A_HEAD_DIM), 1.0),
        'state_conv': nrm(ks[4], (DEPTH, DEC_BATCH, CONV_W - 1, CONV_DIM), 1.0),
        'state_delta': nrm(ks[5], (DEPTH, DEC_BATCH, B_HEADS, B_HEAD_DIM, B_HEAD_DIM), 0.1),
        'meta_tokens': nrm(ks[6], (N_META, D_MODEL), 1.0),
        'norm_mix_pre': 1.0 + nrm(ks[7], (DEPTH, D_MODEL), 0.02),
        'norm_mix_post': 1.0 + nrm(ks[8], (DEPTH, D_MODEL), 0.02),
        'norm_mlp_pre': 1.0 + nrm(ks[9], (DEPTH, D_MODEL), 0.02),
        'norm_mlp_post': 1.0 + nrm(ks[10], (DEPTH, D_MODEL), 0.02),
        'w_in': nrm(ks[11], (DEPTH, D_MODEL, IN_DIM), D_MODEL ** -0.5),
        'sinks': nrm(ks[12], (DEPTH, A_HEADS), 0.5),
        'conv_w': nrm(ks[13], (DEPTH, CONV_W, CONV_DIM), CONV_W ** -0.5),
        'a_log': jnp.log(jax.random.uniform(ks[14], (DEPTH, B_HEADS), f32, 1.0, 16.0)),
        'dt_bias': dt + jnp.log(-jnp.expm1(-dt)),
        'delta_norm': 1.0 + nrm(ks[15], (DEPTH, B_HEAD_DIM), 0.02),
        'w_branch_a': nrm(ks[16], (DEPTH, QA_DIM, D_MODEL), QA_DIM ** -0.5),
        'w_branch_b': nrm(ks[17], (DEPTH, QB_DIM, D_MODEL), QB_DIM ** -0.5),
        'w_out': nrm(ks[18], (DEPTH, D_MODEL, D_MODEL), D_MODEL ** -0.5),
        'w_up': nrm(ks[19], (DEPTH, D_MODEL, D_FF), D_MODEL ** -0.5),
        'w_down': nrm(ks[21], (DEPTH, D_FF, D_MODEL), D_FF ** -0.5),
    }


def reference(x_prompt, x_sample, cache_win_k, cache_win_v, state_conv, state_delta, meta_tokens,
              norm_mix_pre, norm_mix_post, norm_mlp_pre, norm_mlp_post, w_in, sinks, conv_w,
              a_log, dt_bias, delta_norm, w_branch_a, w_branch_b, w_out, w_up, w_down):
    layer_w = (norm_mix_pre, norm_mix_post, norm_mlp_pre, norm_mlp_post, w_in, sinks, conv_w,
               a_log, dt_bias, delta_norm, w_branch_a, w_branch_b, w_out, w_up, w_down)
    Bn = x_prompt.shape[0]
    hp = jnp.concatenate([jnp.broadcast_to(meta_tokens.astype(x_prompt.dtype)[None], (Bn, N_META, D_MODEL)), x_prompt], axis=1)
    L = hp.shape[1]
    pos_p = jnp.arange(L)
    hs = x_sample
    pos_s = PAST_LEN + jnp.arange(hs.shape[1])
    front_pad = (-N_META) % DELTA_CHUNK
    pk, pv, pc, pd, sk, sv, sc, sd = [], [], [], [], [], [], [], []
    for l in range(DEPTH):
        wl = [w[l] for w in layer_w]
        conv0 = jnp.zeros((Bn, CONV_W - 1, CONV_DIM), hp.dtype)
        s0 = jnp.zeros((Bn, B_HEADS, B_HEAD_DIM, B_HEAD_DIM), hp.dtype)
        hp, k_p, v_p, c_p, d_p = _layer(hp, pos_p, attend_prompt, conv0, s0, front_pad, DELTA_CHUNK, *wl)
        att_s = functools.partial(attend_sample, cache_k=cache_win_k[l], cache_v=cache_win_v[l])
        hs, k_s, v_s, c_s, d_s = _layer(hs, pos_s, att_s, state_conv[l], state_delta[l], 0, hs.shape[1], *wl)
        pk.append(k_p); pv.append(v_p); pc.append(c_p); pd.append(d_p)
        sk.append(k_s); sv.append(v_s); sc.append(c_s); sd.append(d_s)
    return (hp[:, N_META:], hs,
            jnp.stack(pk), jnp.stack(pv), jnp.stack(pc), jnp.stack(pd),
            jnp.stack(sk), jnp.stack(sv), jnp.stack(sc), jnp.stack(sd))
```

```python
import functools

import jax
import jax.numpy as jnp
from jax import lax
from jax.experimental import pallas as pl
from jax.experimental.pallas import tpu as pltpu

D_MODEL = 4096
PAST_LEN = 8192
N_META = 16
WINDOW = 128
A_HEADS = 32
A_KV_HEADS = 8
A_GROUP = A_HEADS // A_KV_HEADS
A_HEAD_DIM = 64
ROT_DIM = A_HEAD_DIM // 4
ROPE_THETA = 500000.0
B_HEADS = 16
B_HEAD_DIM = 128
CONV_W = 4
CHUNK = 64
D_FF = 4 * D_MODEL
EPS = 1e-6

QA_DIM = A_HEADS * A_HEAD_DIM
KVA_DIM = A_KV_HEADS * A_HEAD_DIM
QB_DIM = B_HEADS * B_HEAD_DIM
CONV_DIM = 3 * QB_DIM
FRONT = (-N_META) % CHUNK

COL_CONV = 0
COL_ZB = COL_CONV + CONV_DIM
COL_QA = COL_ZB + QB_DIM
COL_GA = COL_QA + QA_DIM
COL_GB = COL_GA + D_MODEL
COL_KA = COL_GB + D_MODEL
COL_VA = COL_KA + KVA_DIM
MAIN_DIM = COL_VA + KVA_DIM
SRC_QA = 0
SRC_KA = SRC_QA + QA_DIM
SRC_VA = SRC_KA + KVA_DIM
SRC_CONV = SRC_VA + KVA_DIM
SRC_ZB = SRC_CONV + CONV_DIM
SRC_BETA = SRC_ZB + QB_DIM
SRC_ALPHA = SRC_BETA + B_HEADS
SRC_GA = SRC_ALPHA + B_HEADS
SRC_GB = SRC_GA + D_MODEL
LANES = 128
SUBLANES = 8
HIST_ROWS = 8
PREV_ROWS = 16
NEG = -1e30
VMEM_LIMIT = 56 * 1024 * 1024
HI = lax.Precision.HIGHEST
F32 = jnp.float32
BF16 = jnp.bfloat16


def _pick(n, target, align):
    best = None
    for t in range(align, min(n, target) + 1, align):
        if n % t == 0:
            best = t
    return best if best is not None else n


def _params(sem):
    return pltpu.CompilerParams(dimension_semantics=sem, vmem_limit_bytes=VMEM_LIMIT)


def _rms(x, w):
    return x * lax.rsqrt(jnp.mean(x * x, axis=-1, keepdims=True) + EPS) * w


def _rmsnorm_kernel(x_ref, w_ref, o_ref):
    o_ref[...] = _rms(x_ref[...], w_ref[...]).astype(o_ref.dtype)


def _rmsnorm(x, w, out_dtype):
    m, d = x.shape
    tr = _pick(m, 256, 16)
    return pl.pallas_call(
        _rmsnorm_kernel,
        grid=(m // tr,),
        in_specs=[pl.BlockSpec((tr, d), lambda i: (i, 0)),
                  pl.BlockSpec((1, d), lambda i: (0, 0))],
        out_specs=pl.BlockSpec((tr, d), lambda i: (i, 0)),
        out_shape=jax.ShapeDtypeStruct((m, d), out_dtype),
        compiler_params=_params(("parallel",)),
        name="rmsnorm",
    )(x, w.reshape(1, d))


def _post_pre_kernel(h_ref, y_ref, wpost_ref, wpre_ref, h1_ref, hn_ref):
    h1 = h_ref[...] + _rms(y_ref[...], wpost_ref[...])
    h1_ref[...] = h1
    hn_ref[...] = _rms(h1, wpre_ref[...]).astype(hn_ref.dtype)


def _post_pre(h, y, w_post, w_pre):
    m, d = h.shape
    tr = _pick(m, 256, 16)
    row = pl.BlockSpec((tr, d), lambda i: (i, 0))
    vec = pl.BlockSpec((1, d), lambda i: (0, 0))
    return pl.pallas_call(
        _post_pre_kernel,
        grid=(m // tr,),
        in_specs=[row, row, vec, vec],
        out_specs=[row, row],
        out_shape=[jax.ShapeDtypeStruct((m, d), F32), jax.ShapeDtypeStruct((m, d), BF16)],
        compiler_params=_params(("parallel",)),
        name="post_pre_norm",
    )(h, y, w_post.reshape(1, d), w_pre.reshape(1, d))


def _post_kernel(h_ref, y_ref, wpost_ref, o_ref):
    o_ref[...] = h_ref[...] + _rms(y_ref[...], wpost_ref[...])


def _post(h, y, w_post):
    m, d = h.shape
    tr = _pick(m, 256, 16)
    row = pl.BlockSpec((tr, d), lambda i: (i, 0))
    vec = pl.BlockSpec((1, d), lambda i: (0, 0))
    return pl.pallas_call(
        _post_kernel,
        grid=(m // tr,),
        in_specs=[row, row, vec],
        out_specs=row,
        out_shape=jax.ShapeDtypeStruct((m, d), F32),
        compiler_params=_params(("parallel",)),
        name="post_norm",
    )(h, y, w_post.reshape(1, d))


def _mm_kernel(x_ref, w_ref, o_ref, acc_ref, *, relu2):
    k = pl.program_id(2)

    @pl.when(k == 0)
    def _():
        acc_ref[...] = jnp.zeros_like(acc_ref)

    acc_ref[...] += jnp.dot(x_ref[...], w_ref[...], preferred_element_type=F32)

    @pl.when(k == pl.num_programs(2) - 1)
    def _():
        acc = acc_ref[...]
        if relu2:
            acc = jnp.square(jnp.maximum(acc, 0.0))
        o_ref[...] = acc.astype(o_ref.dtype)


def _matmul(x, w, out_dtype, relu2=False, name="matmul"):
    m, kd = x.shape
    n = w.shape[1]
    tm = _pick(m, 1280, 16)
    tn = _pick(n, 1024, LANES)
    tk = _pick(kd, 1024, LANES)
    return pl.pallas_call(
        functools.partial(_mm_kernel, relu2=relu2),
        grid=(m // tm, n // tn, kd // tk),
        in_specs=[pl.BlockSpec((tm, tk), lambda i, j, k: (i, k)),
                  pl.BlockSpec((tk, tn), lambda i, j, k: (k, j))],
        out_specs=pl.BlockSpec((tm, tn), lambda i, j, k: (i, j)),
        out_shape=jax.ShapeDtypeStruct((m, n), out_dtype),
        scratch_shapes=[pltpu.VMEM((tm, tn), F32)],
        compiler_params=_params(("parallel", "parallel", "arbitrary")),
        name=name,
    )(x, w)


def _merge_kernel(oa_ref, ob_ref, wa_ref, wb_ref, ga_ref, gb_ref, o_ref, acca_ref, accb_ref):
    k = pl.program_id(2)

    @pl.when(k == 0)
    def _():
        acca_ref[...] = jnp.zeros_like(acca_ref)
        accb_ref[...] = jnp.zeros_like(accb_ref)

    acca_ref[...] += jnp.dot(oa_ref[...], wa_ref[...], preferred_element_type=F32)
    accb_ref[...] += jnp.dot(ob_ref[...], wb_ref[...], preferred_element_type=F32)

    @pl.when(k == pl.num_programs(2) - 1)
    def _():
        ga = jax.nn.sigmoid(ga_ref[...].astype(F32))
        gb = jax.nn.sigmoid(gb_ref[...].astype(F32))
        o_ref[...] = (ga * acca_ref[...] + gb * accb_ref[...]).astype(o_ref.dtype)


def _merge(oa, ob, wa, wb, z):
    m, kd = oa.shape
    n = wa.shape[1]
    tm = _pick(m, 1280, 16)
    tn = _pick(n, 1024, LANES)
    tk = _pick(kd, 1024, LANES)
    ga0, gb0 = COL_GA // tn, COL_GB // tn
    return pl.pallas_call(
        _merge_kernel,
        grid=(m // tm, n // tn, kd // tk),
        in_specs=[pl.BlockSpec((tm, tk), lambda i, j, k: (i, k)),
                  pl.BlockSpec((tm, tk), lambda i, j, k: (i, k)),
                  pl.BlockSpec((tk, tn), lambda i, j, k: (k, j)),
                  pl.BlockSpec((tk, tn), lambda i, j, k: (k, j)),
                  pl.BlockSpec((tm, tn), lambda i, j, k: (i, ga0 + j)),
                  pl.BlockSpec((tm, tn), lambda i, j, k: (i, gb0 + j))],
        out_specs=pl.BlockSpec((tm, tn), lambda i, j, k: (i, j)),
        out_shape=jax.ShapeDtypeStruct((m, n), BF16),
        scratch_shapes=[pltpu.VMEM((tm, tn), F32), pltpu.VMEM((tm, tn), F32)],
        compiler_params=_params(("parallel", "parallel", "arbitrary")),
        name="branch_merge",
    )(oa, ob, wa, wb, z, z)


def _rope_tables(pos):
    half = ROT_DIM // 2
    inv_freq = ROPE_THETA ** (-jnp.arange(half, dtype=F32) * (2.0 / ROT_DIM))
    ang = pos.astype(F32)[:, None] * inv_freq[None, :]
    cos, sin = jnp.cos(ang), jnp.sin(ang)
    rows = pos.shape[0]
    rest = A_HEAD_DIM - ROT_DIM
    one = jnp.ones((rows, rest), F32)
    zero = jnp.zeros((rows, rest), F32)
    zh = jnp.zeros((rows, half), F32)
    reps = LANES // A_HEAD_DIM
    c = jnp.tile(jnp.concatenate([cos, cos, one], axis=1), (1, reps))
    s1 = jnp.tile(jnp.concatenate([-sin, zh, zero], axis=1), (1, reps))
    s2 = jnp.tile(jnp.concatenate([zh, sin, zero], axis=1), (1, reps))
    return c, s1, s2


def _rope_kernel(q_ref, k_ref, c_ref, s1_ref, s2_ref, qo_ref, ko_ref):
    half = ROT_DIM // 2
    c, s1, s2 = c_ref[...], s1_ref[...], s2_ref[...]

    def rot(x):
        return (x * c + pltpu.roll(x, LANES - half, 1) * s1 + pltpu.roll(x, half, 1) * s2)

    scale = A_HEAD_DIM ** -0.5
    for j in range(QA_DIM // LANES):
        sl = slice(j * LANES, (j + 1) * LANES)
        qo_ref[:, sl] = (rot(q_ref[:, sl].astype(F32)) * scale).astype(qo_ref.dtype)
    for j in range(KVA_DIM // LANES):
        sl = slice(j * LANES, (j + 1) * LANES)
        ko_ref[:, sl] = rot(k_ref[:, sl].astype(F32))


def _rope(z, tables):
    m = z.shape[0]
    tr = _pick(m, 256, 16)
    tab = pl.BlockSpec((tr, LANES), lambda i: (i, 0))
    return pl.pallas_call(
        _rope_kernel,
        grid=(m // tr,),
        in_specs=[pl.BlockSpec((tr, QA_DIM), lambda i: (i, COL_QA // QA_DIM)),
                  pl.BlockSpec((tr, KVA_DIM), lambda i: (i, COL_KA // KVA_DIM)),
                  tab, tab, tab],
        out_specs=[pl.BlockSpec((tr, QA_DIM), lambda i: (i, 0)),
                   pl.BlockSpec((tr, KVA_DIM), lambda i: (i, 0))],
        out_shape=[jax.ShapeDtypeStruct((m, QA_DIM), BF16),
                   jax.ShapeDtypeStruct((m, KVA_DIM), F32)],
        compiler_params=_params(("parallel",)),
        name="rope",
    )(z, z, *tables)


def _attend_heads(q_of, k_all, v_all, mask, sinks_ref, write, rows):
    mask_g = jnp.concatenate([mask] * A_GROUP, axis=0)
    for g in range(A_KV_HEADS):
        cs = slice(g * A_HEAD_DIM, (g + 1) * A_HEAD_DIM)
        kh, vh = k_all[:, cs], v_all[:, cs]
        qg = jnp.concatenate([q_of(g * A_GROUP + j) for j in range(A_GROUP)], axis=0)
        sk = jnp.concatenate(
            [jnp.full((rows, 1), sinks_ref[g * A_GROUP + j], F32) for j in range(A_GROUP)], axis=0)
        s = lax.dot_general(qg, kh, (((1,), (1,)), ((), ())), preferred_element_type=F32)
        s = jnp.where(mask_g, s, NEG)
        mx = jnp.maximum(jnp.max(s, axis=-1, keepdims=True), sk)
        p = jnp.exp(s - mx)
        den = jnp.sum(p, axis=-1, keepdims=True) + jnp.exp(sk - mx)
        o = jnp.dot(p.astype(BF16), vh, preferred_element_type=F32) / den
        for j in range(A_GROUP):
            write(g * A_GROUP + j, o[j * rows:(j + 1) * rows])


def _attn_prompt_kernel(sinks_ref, q_ref, kp_ref, kc_ref, vp_ref, vc_ref, o_ref):
    i = pl.program_id(1)
    blk = WINDOW
    k_all = jnp.concatenate([kp_ref[...], kc_ref[...]], axis=0).astype(BF16)
    v_all = jnp.concatenate([vp_ref[...], vc_ref[...]], axis=0).astype(BF16)
    qrow = i * blk + lax.broadcasted_iota(jnp.int32, (blk, 2 * blk), 0)
    krow = (i - 1) * blk + lax.broadcasted_iota(jnp.int32, (blk, 2 * blk), 1)
    diff = qrow - krow
    mask = (diff >= 0) & (diff <= WINDOW) & (krow >= FRONT)

    def q_of(h):
        return q_ref[:, h * A_HEAD_DIM:(h + 1) * A_HEAD_DIM]

    def write(h, o):
        o_ref[:, h * A_HEAD_DIM:(h + 1) * A_HEAD_DIM] = o.astype(o_ref.dtype)

    _attend_heads(q_of, k_all, v_all, mask, sinks_ref, write, blk)


def _attn_prompt(qr, kr, z, sinks, nb, lp):
    blk = WINDOW
    nblk = lp // blk
    vcol = COL_VA // KVA_DIM

    def cur(b, i):
        return b * nblk + i

    def prev(b, i):
        return b * nblk + jnp.maximum(i - 1, 0)

    return pl.pallas_call(
        _attn_prompt_kernel,
        grid=(nb, nblk),
        in_specs=[pl.BlockSpec(memory_space=pltpu.SMEM),
                  pl.BlockSpec((blk, QA_DIM), lambda b, i: (cur(b, i), 0)),
                  pl.BlockSpec((blk, KVA_DIM), lambda b, i: (prev(b, i), 0)),
                  pl.BlockSpec((blk, KVA_DIM), lambda b, i: (cur(b, i), 0)),
                  pl.BlockSpec((blk, KVA_DIM), lambda b, i: (prev(b, i), vcol)),
                  pl.BlockSpec((blk, KVA_DIM), lambda b, i: (cur(b, i), vcol))],
        out_specs=pl.BlockSpec((blk, QA_DIM), lambda b, i: (cur(b, i), 0)),
        out_shape=jax.ShapeDtypeStruct((nb * lp, QA_DIM), BF16),
        compiler_params=_params(("parallel", "parallel")),
        name="attn_prompt",
    )(sinks, qr, kr, kr, z, z)


SAMPLE_ROWS = 8
SAMPLE_KEYS = 256


def _attn_sample_kernel(sinks_ref, q_ref, kn_ref, vn_ref, ck_ref, cv_ref, o_ref, ko_ref, vo_ref,
                        kall_ref, vall_ref, *, bb, n_buf, t_new):
    pad = SAMPLE_KEYS - n_buf - SAMPLE_ROWS
    qrow = lax.broadcasted_iota(jnp.int32, (SAMPLE_ROWS, SAMPLE_KEYS), 0)
    kcol = lax.broadcasted_iota(jnp.int32, (SAMPLE_ROWS, SAMPLE_KEYS), 1)
    diff = qrow + n_buf - kcol
    mask = (diff >= 0) & (diff <= WINDOW) & (kcol < n_buf + t_new)
    for b in range(bb):
        kall_ref[0:n_buf, :] = ck_ref[b]
        kall_ref[n_buf:n_buf + SAMPLE_ROWS, :] = kn_ref[b]
        kall_ref[n_buf + SAMPLE_ROWS:, :] = jnp.zeros((pad, KVA_DIM), F32)
        vall_ref[0:n_buf, :] = cv_ref[b]
        vall_ref[n_buf:n_buf + SAMPLE_ROWS, :] = vn_ref[b]
        vall_ref[n_buf + SAMPLE_ROWS:, :] = jnp.zeros((pad, KVA_DIM), F32)
        ko_ref[b] = kall_ref[t_new:t_new + n_buf, :]
        vo_ref[b] = vall_ref[t_new:t_new + n_buf, :]

        def q_of(h, b=b):
            return q_ref[b, :, h * A_HEAD_DIM:(h + 1) * A_HEAD_DIM].astype(BF16)

        def write(h, o, b=b):
            o_ref[b, :, h * A_HEAD_DIM:(h + 1) * A_HEAD_DIM] = o.astype(o_ref.dtype)

        _attend_heads(q_of, kall_ref[...].astype(BF16), vall_ref[...].astype(BF16), mask,
                      sinks_ref, write, SAMPLE_ROWS)


def _attn_sample(qn, kn, vn, cache_k, cache_v, sinks, t_new):
    bd, n_buf, _ = cache_k.shape
    bb = _pick(bd, 4, 1)
    blk3 = lambda r, c: pl.BlockSpec((bb, r, c), lambda i: (i, 0, 0))
    return pl.pallas_call(
        functools.partial(_attn_sample_kernel, bb=bb, n_buf=n_buf, t_new=t_new),
        grid=(bd // bb,),
        in_specs=[pl.BlockSpec(memory_space=pltpu.SMEM),
                  blk3(SAMPLE_ROWS, QA_DIM), blk3(SAMPLE_ROWS, KVA_DIM), blk3(SAMPLE_ROWS, KVA_DIM),
                  blk3(n_buf, KVA_DIM), blk3(n_buf, KVA_DIM)],
        out_specs=[blk3(SAMPLE_ROWS, QA_DIM), blk3(n_buf, KVA_DIM), blk3(n_buf, KVA_DIM)],
        out_shape=[jax.ShapeDtypeStruct((bd, SAMPLE_ROWS, QA_DIM), BF16),
                   jax.ShapeDtypeStruct((bd, n_buf, KVA_DIM), F32),
                   jax.ShapeDtypeStruct((bd, n_buf, KVA_DIM), F32)],
        scratch_shapes=[pltpu.VMEM((SAMPLE_KEYS, KVA_DIM), F32),
                        pltpu.VMEM((SAMPLE_KEYS, KVA_DIM), F32)],
        compiler_params=_params(("parallel",)),
        name="attn_sample",
    )(sinks, qn, kn, vn, cache_k, cache_v)


def _softplus(x):
    return jnp.maximum(x, 0.0) + jnp.log(1.0 + jnp.exp(-jnp.abs(x)))


def _unit_lower_inverse(m):
    c = m.shape[0]
    r = lax.broadcasted_iota(jnp.int32, (c, c), 0)
    q = lax.broadcasted_iota(jnp.int32, (c, c), 1)
    eye = (r == q).astype(F32)
    dot = functools.partial(jnp.dot, precision=HI, preferred_element_type=F32)
    def blk(a, size):
        return jnp.right_shift(a, size.bit_length() - 1)

    base = SUBLANES
    n1 = -jnp.where(blk(r, base) == blk(q, base), m, 0.0)
    n2 = dot(n1, n1)
    n4 = dot(n2, n2)
    t = dot(dot(eye + n1, eye + n2), eye + n4)
    size = base
    while size < c:
        off = jnp.where((blk(r, 2 * size) == blk(q, 2 * size)) & (blk(r, size) != blk(q, size)), m, 0.0)
        t = t - dot(t, dot(off, t))
        size *= 2
    return t


def _delta_kernel(ab_ref, abt_ref, xq_ref, xk_ref, xv_ref, pq_ref, pk_ref, pv_ref, hist_ref,
                  zb_ref, s0_ref, convw_ref, arow_ref, acol_ref, dnorm_ref,
                  ob_ref, sfin_ref,
                  s_ref, xx_ref, qkv_ref, colg_ref, colb_ref, rowg_ref, *, front, n_valid):
    c = pl.program_id(1)
    C = CHUNK

    @pl.when(c == 0)
    def _():
        s_ref[...] = s0_ref[0]
        xx_ref[0:HIST_ROWS, :] = hist_ref[0]

    @pl.when(c > 0)
    def _():
        lo = PREV_ROWS - HIST_ROWS
        xx_ref[0:HIST_ROWS, 0:QB_DIM] = pq_ref[lo:, :].astype(F32)
        xx_ref[0:HIST_ROWS, QB_DIM:2 * QB_DIM] = pk_ref[lo:, :].astype(F32)
        xx_ref[0:HIST_ROWS, 2 * QB_DIM:] = pv_ref[lo:, :].astype(F32)

    xx_ref[HIST_ROWS:, 0:QB_DIM] = xq_ref[...].astype(F32)
    xx_ref[HIST_ROWS:, QB_DIM:2 * QB_DIM] = xk_ref[...].astype(F32)
    xx_ref[HIST_ROWS:, 2 * QB_DIM:] = xv_ref[...].astype(F32)

    pos_col = c * C + lax.broadcasted_iota(jnp.int32, (C, 1), 0)
    valid_col = ((pos_col >= front) & (pos_col < front + n_valid)).astype(F32)
    pos_row = c * C + lax.broadcasted_iota(jnp.int32, (1, LANES), 1)
    valid_row = ((pos_row >= front) & (pos_row < front + n_valid)).astype(F32)

    first = HIST_ROWS - (CONV_W - 1)
    y = xx_ref[first:first + C, :] * convw_ref[0:1, :]
    for j in range(1, CONV_W):
        y = y + xx_ref[first + j:first + j + C, :] * convw_ref[j:j + 1, :]
    y = y * jax.nn.sigmoid(y)
    qkv_ref[...] = y * valid_col

    ab = ab_ref[...]
    beta = jax.nn.sigmoid(ab)
    g_col = -jnp.exp(arow_ref[0:1, :]) * _softplus(ab + arow_ref[1:2, :]) * valid_col
    g_row = -jnp.exp(acol_ref[:, 0:1]) * _softplus(abt_ref[0, 0] + acol_ref[:, 1:2]) * valid_row
    ri = lax.broadcasted_iota(jnp.int32, (C, C), 0)
    ci = lax.broadcasted_iota(jnp.int32, (C, C), 1)
    tri = ri >= ci
    strict = ri > ci
    big_r = lax.broadcasted_iota(jnp.int32, (LANES, LANES), 0)
    big_c = lax.broadcasted_iota(jnp.int32, (LANES, LANES), 1)
    cum_col = jnp.dot(tri.astype(F32), g_col, precision=HI, preferred_element_type=F32)
    cum_row = jnp.dot(g_row, (big_r <= big_c).astype(F32), precision=HI, preferred_element_type=F32)
    rowg_ref[...] = cum_row
    for h in range(B_HEADS):
        colg_ref[h] = jnp.broadcast_to(cum_col[:, B_HEADS + h:B_HEADS + h + 1], (C, LANES))
        colb_ref[h] = jnp.broadcast_to(beta[:, h:h + 1], (C, LANES))

    dnorm = dnorm_ref[...]
    nt = (((1,), (1,)), ((), ()))
    tn = (((0,), (0,)), ((), ()))

    def head(h, carry):
        off = pl.multiple_of(h * B_HEAD_DIM, B_HEAD_DIM)
        q = qkv_ref[:, pl.ds(off, B_HEAD_DIM)]
        k = qkv_ref[:, pl.ds(QB_DIM + off, B_HEAD_DIM)]
        v = qkv_ref[:, pl.ds(2 * QB_DIM + off, B_HEAD_DIM)]
        q = q * lax.rsqrt(jnp.sum(q * q, axis=-1, keepdims=True) + EPS) * (B_HEAD_DIM ** -0.5)
        k = k * lax.rsqrt(jnp.sum(k * k, axis=-1, keepdims=True) + EPS)
        gb = colg_ref[h]
        bb = colb_ref[h]
        gr = rowg_ref[pl.ds(B_HEADS + h, 1), :][:, 0:C]
        diff = gb[:, 0:C] - gr
        decay = jnp.where(tri, jnp.exp(jnp.where(tri, diff, 0.0)), 0.0)
        eg = jnp.exp(gb)
        g_last = gb[C - 1:C, :]
        kb = k * bb
        m = jnp.where(strict, lax.dot_general(kb, k, nt, preferred_element_type=F32) * decay, 0.0)
        tinv = _unit_lower_inverse(m)
        rhs = jnp.concatenate([v * bb, kb * eg], axis=1)
        sol = jnp.dot(tinv, rhs, precision=HI, preferred_element_type=F32)
        u, w = sol[:, 0:B_HEAD_DIM], sol[:, B_HEAD_DIM:]
        qk = lax.dot_general(q, k, nt, preferred_element_type=F32) * decay
        s = s_ref[h]
        v_new = u - jnp.dot(w, s, preferred_element_type=F32)
        o = (jnp.dot(q * eg, s, preferred_element_type=F32)
             + jnp.dot(qk, v_new, preferred_element_type=F32))
        k_dec = k * jnp.exp(g_last - gb)
        s_ref[h] = s * jnp.exp(g_last) + lax.dot_general(k_dec, v_new, tn, preferred_element_type=F32)
        zb = zb_ref[:, pl.ds(off, B_HEAD_DIM)].astype(F32)
        ob = _rms(o, dnorm) * (zb * jax.nn.sigmoid(zb))
        ob_ref[:, pl.ds(off, B_HEAD_DIM)] = ob.astype(ob_ref.dtype)
        return carry

    lax.fori_loop(0, B_HEADS, head, 0)

    @pl.when(c == pl.num_programs(1) - 1)
    def _():
        sfin_ref[0] = s_ref[...]


def _delta(zsrc, ab, abt, hist, s0, conv_w, arow, acol, dnorm, *, nseq, n_chunks, rows_per_seq,
           front, n_valid):
    C = CHUNK
    cps = rows_per_seq // C
    ppc = C // PREV_ROWS

    def cur(b, c):
        return b * cps + c

    def prev(b, c):
        return jnp.maximum((b * cps + c) * ppc - 1, 0)

    def zcol(j):
        return pl.BlockSpec((C, QB_DIM), lambda b, c: (cur(b, c), j))

    def pcol(j):
        return pl.BlockSpec((PREV_ROWS, QB_DIM), lambda b, c: (prev(b, c), j))

    whole = lambda a: pl.BlockSpec(a.shape, lambda b, c: (0,) * a.ndim)
    state = pl.BlockSpec((1, B_HEADS, B_HEAD_DIM, B_HEAD_DIM), lambda b, c: (b, 0, 0, 0))
    return pl.pallas_call(
        functools.partial(_delta_kernel, front=front, n_valid=n_valid),
        grid=(nseq, n_chunks),
        in_specs=[pl.BlockSpec((C, LANES), lambda b, c: (cur(b, c), 0)),
                  pl.BlockSpec((1, 1, 2 * B_HEADS, LANES), lambda b, c: (b, c, 0, 0)),
                  zcol(0), zcol(1), zcol(2), pcol(0), pcol(1), pcol(2),
                  pl.BlockSpec((1, HIST_ROWS, CONV_DIM), lambda b, c: (b, 0, 0)),
                  zcol(COL_ZB // QB_DIM), state,
                  whole(conv_w), whole(arow), whole(acol), whole(dnorm)],
        out_specs=[pl.BlockSpec((C, QB_DIM), lambda b, c: (cur(b, c), 0)), state],
        out_shape=[jax.ShapeDtypeStruct((nseq * rows_per_seq, QB_DIM), BF16),
                   jax.ShapeDtypeStruct(s0.shape, F32)],
        scratch_shapes=[pltpu.VMEM((B_HEADS, B_HEAD_DIM, B_HEAD_DIM), F32),
                        pltpu.VMEM((HIST_ROWS + C, CONV_DIM), F32),
                        pltpu.VMEM((C, CONV_DIM), F32),
                        pltpu.VMEM((B_HEADS, C, LANES), F32),
                        pltpu.VMEM((B_HEADS, C, LANES), F32),
                        pltpu.VMEM((2 * B_HEADS, LANES), F32)],
        compiler_params=_params(("parallel", "arbitrary")),
        name="gated_delta",
    )(ab, abt, zsrc, zsrc, zsrc, zsrc, zsrc, zsrc, hist, zsrc, s0, conv_w, arow, acol, dnorm)


def _chunk_transpose(ab, nseq, rows_per_seq):
    cps = rows_per_seq // CHUNK
    t = ab.reshape(nseq, cps, CHUNK, LANES)[..., :2 * B_HEADS]
    t = jnp.swapaxes(t, 2, 3)
    return jnp.pad(t, ((0, 0), (0, 0), (0, 0), (0, LANES - CHUNK)))


def _layer(x_all, tables, nb, lp, n_tok, bd, t_new, cache_k, cache_v, state_conv, state_delta,
           norm_mix_pre, norm_mix_post, norm_mlp_pre, norm_mlp_post, w_in, sinks, conv_w,
           a_log, dt_bias, delta_norm, w_branch_a, w_branch_b, w_out, w_up, w_down):
    m = x_all.shape[0]
    p_rows = nb * lp

    def cols(a, n):
        return w_in[:, a:a + n]

    w_main = jnp.concatenate(
        [cols(SRC_CONV, CONV_DIM), cols(SRC_ZB, QB_DIM), cols(SRC_QA, QA_DIM), cols(SRC_GA, D_MODEL),
         cols(SRC_GB, D_MODEL), cols(SRC_KA, KVA_DIM), cols(SRC_VA, KVA_DIM)], axis=1).astype(BF16)
    w_gate = jnp.pad(cols(SRC_BETA, 2 * B_HEADS), ((0, 0), (0, LANES - 2 * B_HEADS))).astype(BF16)

    hn = _rmsnorm(x_all, norm_mix_pre, BF16)
    z = _matmul(hn, w_main, BF16, name="in_proj")
    ab = _matmul(hn, w_gate, F32, name="gate_proj")

    qr, kr = _rope(z, tables)
    oa_p = _attn_prompt(qr, kr, z, sinks, nb, lp)

    def sample_pad(a):
        a = a[p_rows:].astype(F32).reshape(bd, t_new, a.shape[1])
        return jnp.pad(a, ((0, 0), (0, SAMPLE_ROWS - t_new), (0, 0)))

    n_buf = cache_k.shape[1]
    oa_s, k_s, v_s = _attn_sample(
        sample_pad(qr), sample_pad(kr), sample_pad(z[:, COL_VA:COL_VA + KVA_DIM]),
        cache_k.reshape(bd, n_buf, KVA_DIM), cache_v.reshape(bd, n_buf, KVA_DIM), sinks, t_new)
    oa = jnp.concatenate([oa_p, oa_s[:, :t_new].reshape(bd * t_new, QA_DIM)], axis=0)

    arow = jnp.zeros((2, LANES), F32)
    arow = arow.at[0, B_HEADS:2 * B_HEADS].set(a_log).at[1, B_HEADS:2 * B_HEADS].set(dt_bias)
    acol = jnp.zeros((2 * B_HEADS, LANES), F32)
    acol = acol.at[B_HEADS:, 0].set(a_log).at[B_HEADS:, 1].set(dt_bias)
    dnorm = delta_norm.reshape(1, B_HEAD_DIM)
    n_chunks = (FRONT + n_tok) // CHUNK
    ob_p, d_p = _delta(
        z, ab, _chunk_transpose(ab[:p_rows], nb, lp),
        jnp.zeros((nb, HIST_ROWS, CONV_DIM), F32),
        jnp.zeros((nb, B_HEADS, B_HEAD_DIM, B_HEAD_DIM), F32),
        conv_w, arow, acol, dnorm,
        nseq=nb, n_chunks=n_chunks, rows_per_seq=lp, front=FRONT, n_valid=n_tok)

    def seq_pad(a):
        a = a[p_rows:].reshape(bd, t_new, a.shape[1])
        return jnp.pad(a, ((0, 0), (0, CHUNK - t_new), (0, 0))).reshape(bd * CHUNK, a.shape[2])

    zs = seq_pad(z[:, :COL_ZB + QB_DIM])
    abs_ = seq_pad(ab)
    hist_s = jnp.pad(state_conv, ((0, 0), (HIST_ROWS - (CONV_W - 1), 0), (0, 0)))
    ob_s, d_s = _delta(
        zs, abs_, _chunk_transpose(abs_, bd, CHUNK), hist_s, state_delta,
        conv_w, arow, acol, dnorm,
        nseq=bd, n_chunks=1, rows_per_seq=CHUNK, front=0, n_valid=t_new)
    ob = jnp.concatenate(
        [ob_p, ob_s.reshape(bd, CHUNK, QB_DIM)[:, :t_new].reshape(bd * t_new, QB_DIM)], axis=0)

    merged = _merge(oa, ob, w_branch_a.astype(BF16), w_branch_b.astype(BF16), z)
    y = _matmul(merged, w_out.astype(BF16), F32, name="out_proj")
    h1, hn2 = _post_pre(x_all, y, norm_mix_post, norm_mlp_pre)
    u = _matmul(hn2, w_up.astype(BF16), BF16, relu2=True, name="mlp_up")
    y2 = _matmul(u, w_down.astype(BF16), F32, name="mlp_down")
    h2 = _post(h1, y2, norm_mlp_post)

    zp = z[:p_rows].reshape(nb, lp, MAIN_DIM)
    last = FRONT + n_tok
    k_p = kr[:p_rows].reshape(nb, lp, KVA_DIM)[:, last - WINDOW:last]
    v_p = zp[:, last - WINDOW:last, COL_VA:COL_VA + KVA_DIM].astype(F32)
    c_p = zp[:, last - (CONV_W - 1):last, :CONV_DIM].astype(F32)
    x_s = z[p_rows:, :CONV_DIM].astype(F32).reshape(bd, t_new, CONV_DIM)
    c_s = jnp.concatenate([state_conv, x_s], axis=1)[:, -(CONV_W - 1):]
    return h2, k_p, v_p, c_p, d_p, k_s, v_s, c_s, d_s


def kernel(x_prompt, x_sample, cache_win_k, cache_win_v, state_conv, state_delta, meta_tokens,
           norm_mix_pre, norm_mix_post, norm_mlp_pre, norm_mlp_post, w_in, sinks, conv_w,
           a_log, dt_bias, delta_norm, w_branch_a, w_branch_b, w_out, w_up, w_down):
    nb, seq, d = x_prompt.shape
    bd, t_new, _ = x_sample.shape
    depth = w_in.shape[0]
    assert depth == 1 and t_new <= SAMPLE_ROWS and t_new >= CONV_W - 1
    n_tok = N_META + seq
    assert (FRONT + n_tok) % CHUNK == 0 and n_tok >= WINDOW
    lp = -(-(FRONT + n_tok) // WINDOW) * WINDOW
    p_rows = nb * lp

    meta = jnp.broadcast_to(meta_tokens[None], (nb, N_META, d))
    hp = jnp.concatenate([meta, x_prompt], axis=1)
    hp = jnp.pad(hp, ((0, 0), (FRONT, lp - FRONT - n_tok), (0, 0)))
    x_all = jnp.concatenate([hp.reshape(p_rows, d), x_sample.reshape(bd * t_new, d)], axis=0)
    pos_p = jnp.maximum(jnp.arange(lp) - FRONT, 0)
    pos = jnp.concatenate([jnp.tile(pos_p, nb), jnp.tile(PAST_LEN + jnp.arange(t_new), bd)])
    tables = _rope_tables(pos)

    l = 0
    h2, k_p, v_p, c_p, d_p, k_s, v_s, c_s, d_s = _layer(
        x_all, tables, nb, lp, n_tok, bd, t_new, cache_win_k[l], cache_win_v[l], state_conv[l],
        state_delta[l], norm_mix_pre[l], norm_mix_post[l], norm_mlp_pre[l], norm_mlp_post[l],
        w_in[l], sinks[l], conv_w[l], a_log[l], dt_bias[l], delta_norm[l], w_branch_a[l],
        w_branch_b[l], w_out[l], w_up[l], w_down[l])

    y_prompt = h2[:p_rows].reshape(nb, lp, d)[:, FRONT + N_META:FRONT + n_tok]
    y_sample = h2[p_rows:].reshape(bd, t_new, d)
    n_buf = cache_win_k.shape[2]
    kv = lambda a, n, r: a.reshape(1, n, r, A_KV_HEADS, A_HEAD_DIM)
    return (y_prompt, y_sample,
            kv(k_p, nb, WINDOW), kv(v_p, nb, WINDOW), c_p[None], d_p[None],
            kv(k_s, bd, n_buf), kv(v_s, bd, n_buf), c_s[None], d_s[None])
```

```python
import functools

import jax
import jax.numpy as jnp
from jax import lax
from jax.experimental import pallas as pl
from jax.experimental.pallas import tpu as pltpu

D_MODEL = 4096
PAST_LEN = 8192
N_META = 16
WINDOW = 128
A_HEADS = 32
A_KV_HEADS = 8
A_GROUP = A_HEADS // A_KV_HEADS
A_HEAD_DIM = 64
ROT_DIM = A_HEAD_DIM // 4
ROPE_THETA = 500000.0
B_HEADS = 16
B_HEAD_DIM = 128
CONV_W = 4
CHUNK = 64
D_FF = 4 * D_MODEL
EPS = 1e-6

QA_DIM = A_HEADS * A_HEAD_DIM
KVA_DIM = A_KV_HEADS * A_HEAD_DIM
QB_DIM = B_HEADS * B_HEAD_DIM
CONV_DIM = 3 * QB_DIM
FRONT = (-N_META) % CHUNK

COL_CONV = 0
COL_ZB = COL_CONV + CONV_DIM
COL_QA = COL_ZB + QB_DIM
COL_GA = COL_QA + QA_DIM
COL_GB = COL_GA + D_MODEL
COL_KA = COL_GB + D_MODEL
COL_VA = COL_KA + KVA_DIM
MAIN_DIM = COL_VA + KVA_DIM
SRC_QA = 0
SRC_KA = SRC_QA + QA_DIM
SRC_VA = SRC_KA + KVA_DIM
SRC_CONV = SRC_VA + KVA_DIM
SRC_ZB = SRC_CONV + CONV_DIM
SRC_BETA = SRC_ZB + QB_DIM
SRC_ALPHA = SRC_BETA + B_HEADS
SRC_GA = SRC_ALPHA + B_HEADS
SRC_GB = SRC_GA + D_MODEL
LANES = 128
SUBLANES = 8
HIST_ROWS = 8
PREV_ROWS = 16
NEG = -1e30
VMEM_LIMIT = 56 * 1024 * 1024
HI = lax.Precision.HIGHEST
F32 = jnp.float32
BF16 = jnp.bfloat16


def _pick(n, target, align):
    best = None
    for t in range(align, min(n, target) + 1, align):
        if n % t == 0:
            best = t
    return best if best is not None else n


def _params(sem):
    return pltpu.CompilerParams(dimension_semantics=sem, vmem_limit_bytes=VMEM_LIMIT)


def _rms(x, w):
    return x * lax.rsqrt(jnp.mean(x * x, axis=-1, keepdims=True) + EPS) * w


def _rmsnorm_kernel(x_ref, w_ref, o_ref):
    o_ref[...] = _rms(x_ref[...], w_ref[...]).astype(o_ref.dtype)


def _rmsnorm(x, w, out_dtype):
    m, d = x.shape
    tr = _pick(m, 256, 16)
    return pl.pallas_call(
        _rmsnorm_kernel,
        grid=(m // tr,),
        in_specs=[pl.BlockSpec((tr, d), lambda i: (i, 0)),
                  pl.BlockSpec((1, d), lambda i: (0, 0))],
        out_specs=pl.BlockSpec((tr, d), lambda i: (i, 0)),
        out_shape=jax.ShapeDtypeStruct((m, d), out_dtype),
        compiler_params=_params(("parallel",)),
        name="rmsnorm",
    )(x, w.reshape(1, d))


def _post_pre_kernel(h_ref, y_ref, wpost_ref, wpre_ref, h1_ref, hn_ref):
    h1 = h_ref[...] + _rms(y_ref[...], wpost_ref[...])
    h1_ref[...] = h1
    hn_ref[...] = _rms(h1, wpre_ref[...]).astype(hn_ref.dtype)


def _post_pre(h, y, w_post, w_pre):
    m, d = h.shape
    tr = _pick(m, 256, 16)
    row = pl.BlockSpec((tr, d), lambda i: (i, 0))
    vec = pl.BlockSpec((1, d), lambda i: (0, 0))
    return pl.pallas_call(
        _post_pre_kernel,
        grid=(m // tr,),
        in_specs=[row, row, vec, vec],
        out_specs=[row, row],
        out_shape=[jax.ShapeDtypeStruct((m, d), F32), jax.ShapeDtypeStruct((m, d), BF16)],
        compiler_params=_params(("parallel",)),
        name="post_pre_norm",
    )(h, y, w_post.reshape(1, d), w_pre.reshape(1, d))


def _post_kernel(h_ref, y_ref, wpost_ref, o_ref):
    o_ref[...] = h_ref[...] + _rms(y_ref[...], wpost_ref[...])


def _post(h, y, w_post):
    m, d = h.shape
    tr = _pick(m, 256, 16)
    row = pl.BlockSpec((tr, d), lambda i: (i, 0))
    vec = pl.BlockSpec((1, d), lambda i: (0, 0))
    return pl.pallas_call(
        _post_kernel,
        grid=(m // tr,),
        in_specs=[row, row, vec],
        out_specs=row,
        out_shape=jax.ShapeDtypeStruct((m, d), F32),
        compiler_params=_params(("parallel",)),
        name="post_norm",
    )(h, y, w_post.reshape(1, d))


def _mm_kernel(x_ref, w_ref, o_ref, acc_ref, *, relu2):
    k = pl.program_id(2)

    @pl.when(k == 0)
    def _():
        acc_ref[...] = jnp.zeros_like(acc_ref)

    acc_ref[...] += jnp.dot(x_ref[...], w_ref[...], preferred_element_type=F32)

    @pl.when(k == pl.num_programs(2) - 1)
    def _():
        acc = acc_ref[...]
        if relu2:
            acc = jnp.square(jnp.maximum(acc, 0.0))
        o_ref[...] = acc.astype(o_ref.dtype)


def _matmul(x, w, out_dtype, relu2=False, name="matmul"):
    m, kd = x.shape
    n = w.shape[1]
    tm = _pick(m, 1280, 16)
    tn = _pick(n, 1024, LANES)
    tk = _pick(kd, 1024, LANES)
    return pl.pallas_call(
        functools.partial(_mm_kernel, relu2=relu2),
        grid=(m // tm, n // tn, kd // tk),
        in_specs=[pl.BlockSpec((tm, tk), lambda i, j, k: (i, k)),
                  pl.BlockSpec((tk, tn), lambda i, j, k: (k, j))],
        out_specs=pl.BlockSpec((tm, tn), lambda i, j, k: (i, j)),
        out_shape=jax.ShapeDtypeStruct((m, n), out_dtype),
        scratch_shapes=[pltpu.VMEM((tm, tn), F32)],
        compiler_params=_params(("parallel", "parallel", "arbitrary")),
        name=name,
    )(x, w)


def _merge_kernel(oa_ref, ob_ref, wa_ref, wb_ref, ga_ref, gb_ref, o_ref, acca_ref, accb_ref):
    k = pl.program_id(2)

    @pl.when(k == 0)
    def _():
        acca_ref[...] = jnp.zeros_like(acca_ref)
        accb_ref[...] = jnp.zeros_like(accb_ref)

    acca_ref[...] += jnp.dot(oa_ref[...], wa_ref[...], preferred_element_type=F32)
    accb_ref[...] += jnp.dot(ob_ref[...], wb_ref[...], preferred_element_type=F32)

    @pl.when(k == pl.num_programs(2) - 1)
    def _():
        ga = jax.nn.sigmoid(ga_ref[...].astype(F32))
        gb = jax.nn.sigmoid(gb_ref[...].astype(F32))
        o_ref[...] = (ga * acca_ref[...] + gb * accb_ref[...]).astype(o_ref.dtype)


def _merge(oa, ob, wa, wb, z):
    m, kd = oa.shape
    n = wa.shape[1]
    tm = _pick(m, 1280, 16)
    tn = _pick(n, 1024, LANES)
    tk = _pick(kd, 1024, LANES)
    ga0, gb0 = COL_GA // tn, COL_GB // tn
    return pl.pallas_call(
        _merge_kernel,
        grid=(m // tm, n // tn, kd // tk),
        in_specs=[pl.BlockSpec((tm, tk), lambda i, j, k: (i, k)),
                  pl.BlockSpec((tm, tk), lambda i, j, k: (i, k)),
                  pl.BlockSpec((tk, tn), lambda i, j, k: (k, j)),
                  pl.BlockSpec((tk, tn), lambda i, j, k: (k, j)),
                  pl.BlockSpec((tm, tn), lambda i, j, k: (i, ga0 + j)),
                  pl.BlockSpec((tm, tn), lambda i, j, k: (i, gb0 + j))],
        out_specs=pl.BlockSpec((tm, tn), lambda i, j, k: (i, j)),
        out_shape=jax.ShapeDtypeStruct((m, n), BF16),
        scratch_shapes=[pltpu.VMEM((tm, tn), F32), pltpu.VMEM((tm, tn), F32)],
        compiler_params=_params(("parallel", "parallel", "arbitrary")),
        name="branch_merge",
    )(oa, ob, wa, wb, z, z)


def _rope_tables(pos):
    half = ROT_DIM // 2
    inv_freq = ROPE_THETA ** (-jnp.arange(half, dtype=F32) * (2.0 / ROT_DIM))
    ang = pos.astype(F32)[:, None] * inv_freq[None, :]
    cos, sin = jnp.cos(ang), jnp.sin(ang)
    rows = pos.shape[0]
    rest = A_HEAD_DIM - ROT_DIM
    one = jnp.ones((rows, rest), F32)
    zero = jnp.zeros((rows, rest), F32)
    zh = jnp.zeros((rows, half), F32)
    reps = LANES // A_HEAD_DIM
    c = jnp.tile(jnp.concatenate([cos, cos, one], axis=1), (1, reps))
    s1 = jnp.tile(jnp.concatenate([-sin, zh, zero], axis=1), (1, reps))
    s2 = jnp.tile(jnp.concatenate([zh, sin, zero], axis=1), (1, reps))
    return c, s1, s2


def _rope_kernel(q_ref, k_ref, c_ref, s1_ref, s2_ref, qo_ref, ko_ref):
    half = ROT_DIM // 2
    c, s1, s2 = c_ref[...], s1_ref[...], s2_ref[...]

    def rot(x):
        return (x * c + pltpu.roll(x, LANES - half, 1) * s1 + pltpu.roll(x, half, 1) * s2)

    scale = A_HEAD_DIM ** -0.5
    for j in range(QA_DIM // LANES):
        sl = slice(j * LANES, (j + 1) * LANES)
        qo_ref[:, sl] = (rot(q_ref[:, sl].astype(F32)) * scale).astype(qo_ref.dtype)
    for j in range(KVA_DIM // LANES):
        sl = slice(j * LANES, (j + 1) * LANES)
        ko_ref[:, sl] = rot(k_ref[:, sl].astype(F32))


def _rope(z, tables):
    m = z.shape[0]
    tr = _pick(m, 256, 16)
    tab = pl.BlockSpec((tr, LANES), lambda i: (i, 0))
    return pl.pallas_call(
        _rope_kernel,
        grid=(m // tr,),
        in_specs=[pl.BlockSpec((tr, QA_DIM), lambda i: (i, COL_QA // QA_DIM)),
                  pl.BlockSpec((tr, KVA_DIM), lambda i: (i, COL_KA // KVA_DIM)),
                  tab, tab, tab],
        out_specs=[pl.BlockSpec((tr, QA_DIM), lambda i: (i, 0)),
                   pl.BlockSpec((tr, KVA_DIM), lambda i: (i, 0))],
        out_shape=[jax.ShapeDtypeStruct((m, QA_DIM), BF16),
                   jax.ShapeDtypeStruct((m, KVA_DIM), F32)],
        compiler_params=_params(("parallel",)),
        name="rope",
    )(z, z, *tables)


def _attend_heads(q_of, k_all, v_all, mask, sinks_ref, write, rows):
    mask_g = jnp.concatenate([mask] * A_GROUP, axis=0)
    for g in range(A_KV_HEADS):
        cs = slice(g * A_HEAD_DIM, (g + 1) * A_HEAD_DIM)
        kh, vh = k_all[:, cs], v_all[:, cs]
        qg = jnp.concatenate([q_of(g * A_GROUP + j) for j in range(A_GROUP)], axis=0)
        sk = jnp.concatenate(
            [jnp.full((rows, 1), sinks_ref[g * A_GROUP + j], F32) for j in range(A_GROUP)], axis=0)
        s = lax.dot_general(qg, kh, (((1,), (1,)), ((), ())), preferred_element_type=F32)
        s = jnp.where(mask_g, s, NEG)
        mx = jnp.maximum(jnp.max(s, axis=-1, keepdims=True), sk)
        p = jnp.exp(s - mx)
        den = jnp.sum(p, axis=-1, keepdims=True) + jnp.exp(sk - mx)
        o = jnp.dot(p.astype(BF16), vh, preferred_element_type=F32) / den
        for j in range(A_GROUP):
            write(g * A_GROUP + j, o[j * rows:(j + 1) * rows])


def _attn_prompt_kernel(sinks_ref, q_ref, kp_ref, kc_ref, vp_ref, vc_ref, o_ref):
    i = pl.program_id(1)
    blk = WINDOW
    k_all = jnp.concatenate([kp_ref[...], kc_ref[...]], axis=0).astype(BF16)
    v_all = jnp.concatenate([vp_ref[...], vc_ref[...]], axis=0).astype(BF16)
    qrow = i * blk + lax.broadcasted_iota(jnp.int32, (blk, 2 * blk), 0)
    krow = (i - 1) * blk + lax.broadcasted_iota(jnp.int32, (blk, 2 * blk), 1)
    diff = qrow - krow
    mask = (diff >= 0) & (diff <= WINDOW) & (krow >= FRONT)

    def q_of(h):
        return q_ref[:, h * A_HEAD_DIM:(h + 1) * A_HEAD_DIM]

    def write(h, o):
        o_ref[:, h * A_HEAD_DIM:(h + 1) * A_HEAD_DIM] = o.astype(o_ref.dtype)

    _attend_heads(q_of, k_all, v_all, mask, sinks_ref, write, blk)


def _attn_prompt(qr, kr, z, sinks, nb, lp):
    blk = WINDOW
    nblk = lp // blk
    vcol = COL_VA // KVA_DIM

    def cur(b, i):
        return b * nblk + i

    def prev(b, i):
        return b * nblk + jnp.maximum(i - 1, 0)

    return pl.pallas_call(
        _attn_prompt_kernel,
        grid=(nb, nblk),
        in_specs=[pl.BlockSpec(memory_space=pltpu.SMEM),
                  pl.BlockSpec((blk, QA_DIM), lambda b, i: (cur(b, i), 0)),
                  pl.BlockSpec((blk, KVA_DIM), lambda b, i: (prev(b, i), 0)),
                  pl.BlockSpec((blk, KVA_DIM), lambda b, i: (cur(b, i), 0)),
                  pl.BlockSpec((blk, KVA_DIM), lambda b, i: (prev(b, i), vcol)),
                  pl.BlockSpec((blk, KVA_DIM), lambda b, i: (cur(b, i), vcol))],
        out_specs=pl.BlockSpec((blk, QA_DIM), lambda b, i: (cur(b, i), 0)),
        out_shape=jax.ShapeDtypeStruct((nb * lp, QA_DIM), BF16),
        compiler_params=_params(("parallel", "parallel")),
        name="attn_prompt",
    )(sinks, qr, kr, kr, z, z)


SAMPLE_ROWS = 8
SAMPLE_KEYS = 256


def _attn_sample_kernel(sinks_ref, q_ref, kn_ref, vn_ref, ck_ref, cv_ref, o_ref, ko_ref, vo_ref,
                        kall_ref, vall_ref, *, bb, n_buf, t_new):
    pad = SAMPLE_KEYS - n_buf - SAMPLE_ROWS
    qrow = lax.broadcasted_iota(jnp.int32, (SAMPLE_ROWS, SAMPLE_KEYS), 0)
    kcol = lax.broadcasted_iota(jnp.int32, (SAMPLE_ROWS, SAMPLE_KEYS), 1)
    diff = qrow + n_buf - kcol
    mask = (diff >= 0) & (diff <= WINDOW) & (kcol < n_buf + t_new)
    for b in range(bb):
        kall_ref[0:n_buf, :] = ck_ref[b]
        kall_ref[n_buf:n_buf + SAMPLE_ROWS, :] = kn_ref[b]
        kall_ref[n_buf + SAMPLE_ROWS:, :] = jnp.zeros((pad, KVA_DIM), F32)
        vall_ref[0:n_buf, :] = cv_ref[b]
        vall_ref[n_buf:n_buf + SAMPLE_ROWS, :] = vn_ref[b]
        vall_ref[n_buf + SAMPLE_ROWS:, :] = jnp.zeros((pad, KVA_DIM), F32)
        ko_ref[b] = kall_ref[t_new:t_new + n_buf, :]
        vo_ref[b] = vall_ref[t_new:t_new + n_buf, :]

        def q_of(h, b=b):
            return q_ref[b, :, h * A_HEAD_DIM:(h + 1) * A_HEAD_DIM].astype(BF16)

        def write(h, o, b=b):
            o_ref[b, :, h * A_HEAD_DIM:(h + 1) * A_HEAD_DIM] = o.astype(o_ref.dtype)

        _attend_heads(q_of, kall_ref[...].astype(BF16), vall_ref[...].astype(BF16), mask,
                      sinks_ref, write, SAMPLE_ROWS)


def _attn_sample(qn, kn, vn, cache_k, cache_v, sinks, t_new):
    bd, n_buf, _ = cache_k.shape
    bb = _pick(bd, 4, 1)
    blk3 = lambda r, c: pl.BlockSpec((bb, r, c), lambda i: (i, 0, 0))
    return pl.pallas_call(
        functools.partial(_attn_sample_kernel, bb=bb, n_buf=n_buf, t_new=t_new),
        grid=(bd // bb,),
        in_specs=[pl.BlockSpec(memory_space=pltpu.SMEM),
                  blk3(SAMPLE_ROWS, QA_DIM), blk3(SAMPLE_ROWS, KVA_DIM), blk3(SAMPLE_ROWS, KVA_DIM),
                  blk3(n_buf, KVA_DIM), blk3(n_buf, KVA_DIM)],
        out_specs=[blk3(SAMPLE_ROWS, QA_DIM), blk3(n_buf, KVA_DIM), blk3(n_buf, KVA_DIM)],
        out_shape=[jax.ShapeDtypeStruct((bd, SAMPLE_ROWS, QA_DIM), BF16),
                   jax.ShapeDtypeStruct((bd, n_buf, KVA_DIM), F32),
                   jax.ShapeDtypeStruct((bd, n_buf, KVA_DIM), F32)],
        scratch_shapes=[pltpu.VMEM((SAMPLE_KEYS, KVA_DIM), F32),
                        pltpu.VMEM((SAMPLE_KEYS, KVA_DIM), F32)],
        compiler_params=_params(("parallel",)),
        name="attn_sample",
    )(sinks, qn, kn, vn, cache_k, cache_v)


TILE = 128
TOK = SUBLANES


def _softplus(x):
    return jnp.maximum(x, 0.0) + jnp.log(1.0 + jnp.exp(-jnp.abs(x)))


def _bf(x):
    return x.astype(BF16)


def _mm(a, b):
    return jnp.dot(_bf(a), _bf(b), preferred_element_type=F32)


def _mm_nt(a, b):
    return lax.dot_general(_bf(a), _bf(b), (((1,), (1,)), ((), ())), preferred_element_type=F32)


def _mm_tn(a, b):
    return lax.dot_general(a, b, (((0,), (0,)), ((), ())), preferred_element_type=F32)


def _tile_masks(block):
    r = lax.broadcasted_iota(jnp.int32, (TILE, TILE), 0)
    q = lax.broadcasted_iota(jnp.int32, (TILE, TILE), 1)

    def same(size):
        sh = size.bit_length() - 1
        return jnp.right_shift(r, sh) == jnp.right_shift(q, sh)

    tri = same(block) & (r >= q)
    strict = same(block) & (r > q)
    levels = []
    size = SUBLANES
    while size < block:
        levels.append(same(2 * size) & jnp.logical_not(same(size)))
        size *= 2
    return tri, strict, same(SUBLANES), levels


def _inverse_minus_eye(m, same8, levels):
    n1 = -jnp.where(same8, m, 0.0)
    n2 = _mm(n1, n1)
    yield
    n4 = _mm(n2, n2)
    n12 = _mm(n1, n2)
    yield
    a = n1 + n2 + n12
    d = a + n4 + _mm(a, n4)
    yield
    for mask in levels:
        off = jnp.where(mask, m, 0.0)
        p = off + _mm(off, d)
        yield
        d = d - (p + _mm(d, p))
        yield
    return d


def _delta_tile(q, k, v, gb, bb, grow, glast, masks):
    tri, strict, same8, levels = masks
    decay = jnp.where(tri, jnp.exp(jnp.where(tri, gb - grow, 0.0)), 0.0)
    eg = jnp.exp(gb)
    kb = k * bb
    aq = _mm_nt(jnp.concatenate([kb, q], axis=0), k)
    yield
    m = jnp.where(strict, aq[:TILE] * decay, 0.0)
    qk = aq[TILE:] * decay
    d = yield from _inverse_minus_eye(m, same8, levels)
    rhs = jnp.concatenate([v * bb, kb * eg], axis=1)
    sol = rhs + _mm(d, rhs)
    yield
    return sol[:, :B_HEAD_DIM], sol[:, B_HEAD_DIM:], qk, q * eg, k * jnp.exp(glast - gb)


def _interleave(gens):
    live = list(gens)
    while live:
        nxt = []
        for g in live:
            try:
                next(g)
                nxt.append(g)
            except StopIteration:
                pass
        live = nxt


def _l2(x):
    return x * lax.rsqrt(jnp.sum(x * x, axis=-1, keepdims=True) + EPS)


def _gated_norm(o, zb, dnorm):
    return _rms(o, dnorm) * (zb * jax.nn.sigmoid(zb))


def _delta_prompt_kernel(ab_ref, abt_ref, xq_ref, xk_ref, xv_ref, pq_ref, pk_ref, pv_ref, hist_ref,
                         zb_ref, s0_ref, convw_ref, arow_ref, acol_ref, dnorm_ref,
                         ob_ref, sfin_ref, s_ref, xx_ref, *, front, n_valid):
    c = pl.program_id(1)

    @pl.when(c == 0)
    def _():
        s_ref[...] = s0_ref[0]
        xx_ref[0:HIST_ROWS, :] = hist_ref[0]

    @pl.when(c > 0)
    def _():
        lo = PREV_ROWS - HIST_ROWS
        xx_ref[0:HIST_ROWS, 0:QB_DIM] = pq_ref[lo:, :].astype(F32)
        xx_ref[0:HIST_ROWS, QB_DIM:2 * QB_DIM] = pk_ref[lo:, :].astype(F32)
        xx_ref[0:HIST_ROWS, 2 * QB_DIM:] = pv_ref[lo:, :].astype(F32)

    xx_ref[HIST_ROWS:, 0:QB_DIM] = xq_ref[...].astype(F32)
    xx_ref[HIST_ROWS:, QB_DIM:2 * QB_DIM] = xk_ref[...].astype(F32)
    xx_ref[HIST_ROWS:, 2 * QB_DIM:] = xv_ref[...].astype(F32)

    pos_col = c * TILE + lax.broadcasted_iota(jnp.int32, (TILE, 1), 0)
    valid_col = ((pos_col >= front) & (pos_col < front + n_valid)).astype(F32)
    pos_row = c * TILE + lax.broadcasted_iota(jnp.int32, (1, TILE), 1)
    valid_row = ((pos_row >= front) & (pos_row < front + n_valid)).astype(F32)

    ab = ab_ref[...]
    beta = jax.nn.sigmoid(ab)
    g_col = -jnp.exp(arow_ref[0:1, :]) * _softplus(ab + arow_ref[1:2, :]) * valid_col
    g_row = -jnp.exp(acol_ref[:, 0:1]) * _softplus(abt_ref[0, 0] + acol_ref[:, 1:2]) * valid_row
    ri = lax.broadcasted_iota(jnp.int32, (TILE, TILE), 0)
    ci = lax.broadcasted_iota(jnp.int32, (TILE, TILE), 1)
    cum_col = jnp.dot((ri >= ci).astype(F32), g_col, precision=HI, preferred_element_type=F32)
    cum_row = jnp.dot(g_row, (ri <= ci).astype(F32), precision=HI, preferred_element_type=F32)
    masks = _tile_masks(TILE)
    dnorm = dnorm_ref[...]
    first = HIST_ROWS - (CONV_W - 1)

    def conv(col):
        cs = slice(col, col + B_HEAD_DIM)
        y = xx_ref[first:first + TILE, cs] * convw_ref[0:1, cs]
        for j in range(1, CONV_W):
            y = y + xx_ref[first + j:first + j + TILE, cs] * convw_ref[j:j + 1, cs]
        return y * jax.nn.sigmoid(y) * valid_col

    def head(h):
        off = h * B_HEAD_DIM
        q = _l2(conv(off)) * (B_HEAD_DIM ** -0.5)
        k = _l2(conv(QB_DIM + off))
        v = conv(2 * QB_DIM + off)
        gb = jnp.broadcast_to(cum_col[:, B_HEADS + h:B_HEADS + h + 1], (TILE, LANES))
        bb = jnp.broadcast_to(beta[:, h:h + 1], (TILE, LANES))
        grow = cum_row[B_HEADS + h:B_HEADS + h + 1, :]
        glast = gb[TILE - 1:TILE, :]
        u, w, qk, qd, kd = yield from _delta_tile(q, k, v, gb, bb, grow, glast, masks)
        s = s_ref[h]
        wq = _mm(jnp.concatenate([w, qd], axis=0), s)
        yield
        v_new = u - wq[:TILE]
        o = wq[TILE:] + _mm(qk, v_new)
        s_ref[h] = s * jnp.exp(glast) + _mm_tn(kd, v_new)
        yield
        zb = zb_ref[:, off:off + B_HEAD_DIM].astype(F32)
        ob_ref[:, off:off + B_HEAD_DIM] = _gated_norm(o, zb, dnorm).astype(ob_ref.dtype)

    _interleave(head(h) for h in range(B_HEADS))

    @pl.when(c == pl.num_programs(1) - 1)
    def _():
        sfin_ref[0] = s_ref[...]


def _delta_prompt(z, ab, abt, hist, s0, conv_w, arow, acol, dnorm, *, nseq, rows_per_seq, front,
                  n_valid):
    tps = rows_per_seq // TILE
    ppt = TILE // PREV_ROWS

    def cur(b, c):
        return b * tps + c

    def prev(b, c):
        return jnp.maximum((b * tps + c) * ppt - 1, 0)

    def zcol(j):
        return pl.BlockSpec((TILE, QB_DIM), lambda b, c: (cur(b, c), j))

    def pcol(j):
        return pl.BlockSpec((PREV_ROWS, QB_DIM), lambda b, c: (prev(b, c), j))

    whole = lambda a: pl.BlockSpec(a.shape, lambda b, c: (0,) * a.ndim)
    state = pl.BlockSpec((1, B_HEADS, B_HEAD_DIM, B_HEAD_DIM), lambda b, c: (b, 0, 0, 0))
    return pl.pallas_call(
        functools.partial(_delta_prompt_kernel, front=front, n_valid=n_valid),
        grid=(nseq, tps),
        in_specs=[pl.BlockSpec((TILE, LANES), lambda b, c: (cur(b, c), 0)),
                  pl.BlockSpec((1, 1, 2 * B_HEADS, TILE), lambda b, c: (b, c, 0, 0)),
                  zcol(0), zcol(1), zcol(2), pcol(0), pcol(1), pcol(2),
                  pl.BlockSpec((1, HIST_ROWS, CONV_DIM), lambda b, c: (b, 0, 0)),
                  zcol(COL_ZB // QB_DIM), state,
                  whole(conv_w), whole(arow), whole(acol), whole(dnorm)],
        out_specs=[pl.BlockSpec((TILE, QB_DIM), lambda b, c: (cur(b, c), 0)), state],
        out_shape=[jax.ShapeDtypeStruct((nseq * rows_per_seq, QB_DIM), BF16),
                   jax.ShapeDtypeStruct(s0.shape, F32)],
        scratch_shapes=[pltpu.VMEM((B_HEADS, B_HEAD_DIM, B_HEAD_DIM), F32),
                        pltpu.VMEM((HIST_ROWS + TILE, CONV_DIM), F32)],
        compiler_params=_params(("parallel", "arbitrary")),
        name="delta_prompt",
    )(ab, abt, z, z, z, z, z, z, hist, z, s0, conv_w, arow, acol, dnorm)


def _delta_sample_kernel(x_ref, ab_ref, abt_ref, hist_ref, s0_ref, convw_ref, arow_ref, alane_ref,
                         dnorm_ref, ob_ref, sout_ref, xx_ref, *, bb, t_new):
    masks = _tile_masks(TOK)
    trow = lax.broadcasted_iota(jnp.int32, (TOK, 1), 0)
    valid_t = (trow < t_new).astype(F32)
    tlane = jnp.bitwise_and(lax.broadcasted_iota(jnp.int32, (TOK, LANES), 1), TOK - 1)
    valid_lane = (tlane < t_new).astype(F32)
    dnorm = dnorm_ref[...]
    first = HIST_ROWS - (CONV_W - 1)

    def stack(a, base):
        return jnp.concatenate(
            [a[:, base + h * B_HEAD_DIM:base + (h + 1) * B_HEAD_DIM] for h in range(B_HEADS)], axis=0)

    def rows_of(col_of):
        return jnp.concatenate(
            [jnp.broadcast_to(col_of(h), (TOK, LANES)) for h in range(B_HEADS)], axis=0)

    def seq(b):
        xx_ref[0:HIST_ROWS, :] = hist_ref[b]
        xx_ref[HIST_ROWS:, :] = x_ref[b, :, 0:CONV_DIM]
        y = xx_ref[first:first + TOK, :] * convw_ref[0:1, :]
        for j in range(1, CONV_W):
            y = y + xx_ref[first + j:first + j + TOK, :] * convw_ref[j:j + 1, :]
        y = y * jax.nn.sigmoid(y) * valid_t
        q = _l2(stack(y, 0)) * (B_HEAD_DIM ** -0.5)
        k = _l2(stack(y, QB_DIM))
        v = stack(y, 2 * QB_DIM)

        ab = ab_ref[b]
        beta = jax.nn.sigmoid(ab)
        g = -jnp.exp(arow_ref[0:1, :]) * _softplus(ab + arow_ref[1:2, :]) * valid_t
        gl = (-jnp.exp(alane_ref[0:1, :]) * _softplus(abt_ref[b] + alane_ref[1:2, :])) * valid_lane
        step = 1
        while step < TOK:
            g = g + jnp.where(trow >= step, pltpu.roll(g, step, 0), 0.0)
            gl = gl + jnp.where(tlane >= step, pltpu.roll(gl, step, 1), 0.0)
            step *= 2
        gb = rows_of(lambda h: g[:, B_HEADS + h:B_HEADS + h + 1])
        bt = rows_of(lambda h: beta[:, h:h + 1])
        glast = rows_of(lambda h: g[TOK - 1:TOK, B_HEADS + h:B_HEADS + h + 1])
        u, w, qk, qd, kd = yield from _delta_tile(q, k, v, gb, bt, gl[0:1, :], glast, masks)

        wq = []
        for h in range(B_HEADS):
            rs = slice(h * TOK, (h + 1) * TOK)
            wq.append(_mm(jnp.concatenate([w[rs], qd[rs]], axis=0), s0_ref[b, h]))
        yield
        v_new = u - jnp.concatenate([x[:TOK] for x in wq], axis=0)
        o = jnp.concatenate([x[TOK:] for x in wq], axis=0) + _mm(qk, v_new)
        yield
        for h in range(B_HEADS):
            rs = slice(h * TOK, (h + 1) * TOK)
            sout_ref[b, h] = (s0_ref[b, h] * jnp.exp(glast[h * TOK:h * TOK + 1, :])
                              + _mm_tn(kd[rs], v_new[rs]))
        ob = _gated_norm(o, stack(x_ref[b], COL_ZB), dnorm)
        for h in range(B_HEADS):
            ob_ref[b, :, h * B_HEAD_DIM:(h + 1) * B_HEAD_DIM] = ob[h * TOK:(h + 1) * TOK]

    _interleave(seq(b) for b in range(bb))


def _delta_sample(xs, ab, abt, hist, s0, conv_w, arow, alane, dnorm, t_new):
    bd = xs.shape[0]
    bb = _pick(bd, 2, 1)
    blk = lambda a: pl.BlockSpec((bb,) + a.shape[1:], lambda i: (i,) + (0,) * (a.ndim - 1))
    whole = lambda a: pl.BlockSpec(a.shape, lambda i: (0,) * a.ndim)
    return pl.pallas_call(
        functools.partial(_delta_sample_kernel, bb=bb, t_new=t_new),
        grid=(bd // bb,),
        in_specs=[blk(xs), blk(ab), blk(abt), blk(hist), blk(s0),
                  whole(conv_w), whole(arow), whole(alane), whole(dnorm)],
        out_specs=[pl.BlockSpec((bb, TOK, QB_DIM), lambda i: (i, 0, 0)), blk(s0)],
        out_shape=[jax.ShapeDtypeStruct((bd, TOK, QB_DIM), F32),
                   jax.ShapeDtypeStruct(s0.shape, F32)],
        scratch_shapes=[pltpu.VMEM((HIST_ROWS + TOK, CONV_DIM), F32)],
        compiler_params=_params(("parallel",)),
        name="delta_sample",
    )(xs, ab, abt, hist, s0, conv_w, arow, alane, dnorm)


def _layer(x_all, tables, nb, lp, n_tok, bd, t_new, cache_k, cache_v, state_conv, state_delta,
           norm_mix_pre, norm_mix_post, norm_mlp_pre, norm_mlp_post, w_in, sinks, conv_w,
           a_log, dt_bias, delta_norm, w_branch_a, w_branch_b, w_out, w_up, w_down):
    m = x_all.shape[0]
    p_rows = nb * lp

    def cols(a, n):
        return w_in[:, a:a + n]

    w_main = jnp.concatenate(
        [cols(SRC_CONV, CONV_DIM), cols(SRC_ZB, QB_DIM), cols(SRC_QA, QA_DIM), cols(SRC_GA, D_MODEL),
         cols(SRC_GB, D_MODEL), cols(SRC_KA, KVA_DIM), cols(SRC_VA, KVA_DIM)], axis=1).astype(BF16)
    w_gate = jnp.pad(cols(SRC_BETA, 2 * B_HEADS), ((0, 0), (0, LANES - 2 * B_HEADS))).astype(BF16)

    hn = _rmsnorm(x_all, norm_mix_pre, BF16)
    z = _matmul(hn, w_main, BF16, name="in_proj")
    ab = _matmul(hn, w_gate, F32, name="gate_proj")

    qr, kr = _rope(z, tables)
    oa_p = _attn_prompt(qr, kr, z, sinks, nb, lp)

    def sample_pad(a):
        a = a[p_rows:].astype(F32).reshape(bd, t_new, a.shape[1])
        return jnp.pad(a, ((0, 0), (0, SAMPLE_ROWS - t_new), (0, 0)))

    n_buf = cache_k.shape[1]
    oa_s, k_s, v_s = _attn_sample(
        sample_pad(qr), sample_pad(kr), sample_pad(z[:, COL_VA:COL_VA + KVA_DIM]),
        cache_k.reshape(bd, n_buf, KVA_DIM), cache_v.reshape(bd, n_buf, KVA_DIM), sinks, t_new)
    oa = jnp.concatenate([oa_p, oa_s[:, :t_new].reshape(bd * t_new, QA_DIM)], axis=0)

    arow = jnp.zeros((2, LANES), F32)
    arow = arow.at[0, B_HEADS:2 * B_HEADS].set(a_log).at[1, B_HEADS:2 * B_HEADS].set(dt_bias)
    acol = jnp.zeros((2 * B_HEADS, LANES), F32)
    acol = acol.at[B_HEADS:, 0].set(a_log).at[B_HEADS:, 1].set(dt_bias)
    alane = jnp.stack([jnp.repeat(a_log, TOK), jnp.repeat(dt_bias, TOK)])
    dnorm = delta_norm.reshape(1, B_HEAD_DIM)
    abt_p = jnp.swapaxes(
        ab[:p_rows].reshape(nb, lp // TILE, TILE, LANES)[..., :2 * B_HEADS], 2, 3)
    ob_p, d_p = _delta_prompt(
        z, ab, abt_p, jnp.zeros((nb, HIST_ROWS, CONV_DIM), F32),
        jnp.zeros((nb, B_HEADS, B_HEAD_DIM, B_HEAD_DIM), F32),
        conv_w, arow, acol, dnorm, nseq=nb, rows_per_seq=lp, front=FRONT, n_valid=n_tok)

    xs = sample_pad(z[:, :COL_ZB + QB_DIM])
    ab_s = sample_pad(ab)
    abt_s = jnp.swapaxes(ab_s[..., B_HEADS:2 * B_HEADS], 1, 2).reshape(bd, 1, B_HEADS * TOK)
    hist_s = jnp.pad(state_conv, ((0, 0), (HIST_ROWS - (CONV_W - 1), 0), (0, 0)))
    ob_s, d_s = _delta_sample(xs, ab_s, abt_s, hist_s, state_delta, conv_w, arow, alane, dnorm, t_new)
    ob = jnp.concatenate(
        [ob_p, ob_s[:, :t_new].reshape(bd * t_new, QB_DIM).astype(BF16)], axis=0)

    merged = _merge(oa, ob, w_branch_a.astype(BF16), w_branch_b.astype(BF16), z)
    y = _matmul(merged, w_out.astype(BF16), F32, name="out_proj")
    h1, hn2 = _post_pre(x_all, y, norm_mix_post, norm_mlp_pre)
    u = _matmul(hn2, w_up.astype(BF16), BF16, relu2=True, name="mlp_up")
    y2 = _matmul(u, w_down.astype(BF16), F32, name="mlp_down")
    h2 = _post(h1, y2, norm_mlp_post)

    zp = z[:p_rows].reshape(nb, lp, MAIN_DIM)
    last = FRONT + n_tok
    k_p = kr[:p_rows].reshape(nb, lp, KVA_DIM)[:, last - WINDOW:last]
    v_p = zp[:, last - WINDOW:last, COL_VA:COL_VA + KVA_DIM].astype(F32)
    c_p = zp[:, last - (CONV_W - 1):last, :CONV_DIM].astype(F32)
    x_s = z[p_rows:, :CONV_DIM].astype(F32).reshape(bd, t_new, CONV_DIM)
    c_s = jnp.concatenate([state_conv, x_s], axis=1)[:, -(CONV_W - 1):]
    return h2, k_p, v_p, c_p, d_p, k_s, v_s, c_s, d_s


def kernel(x_prompt, x_sample, cache_win_k, cache_win_v, state_conv, state_delta, meta_tokens,
           norm_mix_pre, norm_mix_post, norm_mlp_pre, norm_mlp_post, w_in, sinks, conv_w,
           a_log, dt_bias, delta_norm, w_branch_a, w_branch_b, w_out, w_up, w_down):
    nb, seq, d = x_prompt.shape
    bd, t_new, _ = x_sample.shape
    depth = w_in.shape[0]
    assert depth == 1 and CONV_W - 1 <= t_new <= SAMPLE_ROWS == TOK
    n_tok = N_META + seq
    assert n_tok >= WINDOW
    lp = -(-(FRONT + n_tok) // WINDOW) * WINDOW
    p_rows = nb * lp

    meta = jnp.broadcast_to(meta_tokens[None], (nb, N_META, d))
    hp = jnp.concatenate([meta, x_prompt], axis=1)
    hp = jnp.pad(hp, ((0, 0), (FRONT, lp - FRONT - n_tok), (0, 0)))
    x_all = jnp.concatenate([hp.reshape(p_rows, d), x_sample.reshape(bd * t_new, d)], axis=0)
    pos_p = jnp.maximum(jnp.arange(lp) - FRONT, 0)
    pos = jnp.concatenate([jnp.tile(pos_p, nb), jnp.tile(PAST_LEN + jnp.arange(t_new), bd)])
    tables = _rope_tables(pos)

    l = 0
    h2, k_p, v_p, c_p, d_p, k_s, v_s, c_s, d_s = _layer(
        x_all, tables, nb, lp, n_tok, bd, t_new, cache_win_k[l], cache_win_v[l], state_conv[l],
        state_delta[l], norm_mix_pre[l], norm_mix_post[l], norm_mlp_pre[l], norm_mlp_post[l],
        w_in[l], sinks[l], conv_w[l], a_log[l], dt_bias[l], delta_norm[l], w_branch_a[l],
        w_branch_b[l], w_out[l], w_up[l], w_down[l])

    y_prompt = h2[:p_rows].reshape(nb, lp, d)[:, FRONT + N_META:FRONT + n_tok]
    y_sample = h2[p_rows:].reshape(bd, t_new, d)
    n_buf = cache_win_k.shape[2]
    kv = lambda a, n, r: a.reshape(1, n, r, A_KV_HEADS, A_HEAD_DIM)
    return (y_prompt, y_sample,
            kv(k_p, nb, WINDOW), kv(v_p, nb, WINDOW), c_p[None], d_p[None],
            kv(k_s, bd, n_buf), kv(v_s, bd, n_buf), c_s[None], d_s[None])
```

```python
import functools

import jax
import jax.numpy as jnp
from jax import lax
from jax.experimental import pallas as pl
from jax.experimental.pallas import tpu as pltpu

D_MODEL = 4096
PAST_LEN = 8192
N_META = 16
WINDOW = 128
A_HEADS = 32
A_KV_HEADS = 8
A_GROUP = A_HEADS // A_KV_HEADS
A_HEAD_DIM = 64
ROT_DIM = A_HEAD_DIM // 4
ROPE_THETA = 500000.0
B_HEADS = 16
B_HEAD_DIM = 128
CONV_W = 4
CHUNK = 64
D_FF = 4 * D_MODEL
EPS = 1e-6

QA_DIM = A_HEADS * A_HEAD_DIM
KVA_DIM = A_KV_HEADS * A_HEAD_DIM
QB_DIM = B_HEADS * B_HEAD_DIM
CONV_DIM = 3 * QB_DIM
FRONT = (-N_META) % CHUNK

COL_CONV = 0
COL_ZB = COL_CONV + CONV_DIM
COL_QA = COL_ZB + QB_DIM
COL_KA = COL_QA + QA_DIM
COL_VA = COL_KA + KVA_DIM
Z_DIM = COL_VA + KVA_DIM
SRC_QA = 0
SRC_KA = SRC_QA + QA_DIM
SRC_VA = SRC_KA + KVA_DIM
SRC_CONV = SRC_VA + KVA_DIM
SRC_ZB = SRC_CONV + CONV_DIM
SRC_BETA = SRC_ZB + QB_DIM
SRC_ALPHA = SRC_BETA + B_HEADS
SRC_GA = SRC_ALPHA + B_HEADS
SRC_GB = SRC_GA + D_MODEL
LANES = 128
SUBLANES = 8
RB = FRONT + N_META
HIST_ROWS = 8
PREV_ROWS = 16
NEG = -1e30
VMEM_LIMIT = 56 * 1024 * 1024
HI = lax.Precision.HIGHEST
F32 = jnp.float32
BF16 = jnp.bfloat16


def _pick(n, target, align):
    best = None
    for t in range(align, min(n, target) + 1, align):
        if n % t == 0:
            best = t
    return best if best is not None else n


def _params(sem):
    return pltpu.CompilerParams(dimension_semantics=sem, vmem_limit_bytes=VMEM_LIMIT)


def _rms(x, w):
    return x * lax.rsqrt(jnp.mean(x * x, axis=-1, keepdims=True) + EPS) * w


class _Rows:
    def __init__(self, nb, seq, lp, n_sample):
        assert FRONT + N_META == RB and seq % RB == 0 and lp % RB == 0 and n_sample % RB == 0
        self.nb, self.bps, self.seq_blocks = nb, lp // RB, seq // RB
        self.sample_blocks = n_sample // RB
        self.blocks = nb * self.bps + self.sample_blocks

    def prompt_index(self, r):
        b = jnp.minimum(r // self.bps, self.nb - 1)
        s = jnp.where(r < self.nb * self.bps, jnp.clip(r % self.bps - 1, 0, self.seq_blocks - 1),
                      self.seq_blocks - 1)
        return b, s, 0

    def sample_index(self, r):
        return jnp.clip(r - self.nb * self.bps, 0, self.sample_blocks - 1), 0

    def kinds(self, r):
        rb = r % self.bps
        is_p = r < self.nb * self.bps
        return (is_p & (rb == 0), is_p & (rb >= 1) & (rb <= self.seq_blocks),
                is_p & (rb > self.seq_blocks), jnp.logical_not(is_p))

    def in_specs(self, d):
        return [pl.BlockSpec((1, RB, d), self.prompt_index),
                pl.BlockSpec((N_META, d), lambda r: (0, 0)),
                pl.BlockSpec((RB, d), self.sample_index)]

    def gather(self, r, xp_ref, meta_ref, xs_ref, h_ref):
        is_meta, is_seq, is_zero, is_sample = self.kinds(r)

        @pl.when(is_meta)
        def _():
            h_ref[0:FRONT, :] = jnp.zeros((FRONT, h_ref.shape[1]), F32)
            h_ref[FRONT:, :] = meta_ref[...]

        @pl.when(is_seq)
        def _():
            h_ref[...] = xp_ref[0]

        @pl.when(is_zero)
        def _():
            h_ref[...] = jnp.zeros_like(h_ref)

        @pl.when(is_sample)
        def _():
            h_ref[...] = xs_ref[...]


def _norm_in_kernel(xp_ref, meta_ref, xs_ref, w_ref, o_ref, h_ref, *, rows):
    rows.gather(pl.program_id(0), xp_ref, meta_ref, xs_ref, h_ref)
    o_ref[...] = _rms(h_ref[...], w_ref[...]).astype(o_ref.dtype)


def _norm_in(rows, xp, meta, xs, w):
    d = xp.shape[-1]
    row = pl.BlockSpec((RB, d), lambda r: (r, 0))
    vec = pl.BlockSpec((1, d), lambda r: (0, 0))
    return pl.pallas_call(
        functools.partial(_norm_in_kernel, rows=rows),
        grid=(rows.blocks,),
        in_specs=rows.in_specs(d) + [vec],
        out_specs=row,
        out_shape=jax.ShapeDtypeStruct((rows.blocks * RB, d), BF16),
        scratch_shapes=[pltpu.VMEM((RB, d), F32)],
        compiler_params=_params(("parallel",)),
        name="norm_in",
    )(xp, meta, xs, w.reshape(1, d))


def _post_pre_kernel(xp_ref, meta_ref, xs_ref, y_ref, wpost_ref, wpre_ref, h1_ref, hn_ref, h_ref,
                     *, rows):
    rows.gather(pl.program_id(0), xp_ref, meta_ref, xs_ref, h_ref)
    h1 = h_ref[...] + _rms(y_ref[...], wpost_ref[...])
    h1_ref[...] = h1
    hn_ref[...] = _rms(h1, wpre_ref[...]).astype(hn_ref.dtype)


def _post_pre(rows, xp, meta, xs, y, w_post, w_pre):
    d = xp.shape[-1]
    m = rows.blocks * RB
    row = pl.BlockSpec((RB, d), lambda r: (r, 0))
    vec = pl.BlockSpec((1, d), lambda r: (0, 0))
    return pl.pallas_call(
        functools.partial(_post_pre_kernel, rows=rows),
        grid=(rows.blocks,),
        in_specs=rows.in_specs(d) + [row, vec, vec],
        out_specs=[row, row],
        out_shape=[jax.ShapeDtypeStruct((m, d), F32), jax.ShapeDtypeStruct((m, d), BF16)],
        scratch_shapes=[pltpu.VMEM((RB, d), F32)],
        compiler_params=_params(("parallel",)),
        name="post_pre_norm",
    )(xp, meta, xs, y, w_post.reshape(1, d), w_pre.reshape(1, d))


def _post_out_kernel(h_ref, y_ref, wpost_ref, yp_ref, ys_ref, *, rows):
    _, is_seq, _, is_sample = rows.kinds(pl.program_id(0))
    out = h_ref[...] + _rms(y_ref[...], wpost_ref[...])

    @pl.when(is_seq)
    def _():
        yp_ref[0] = out

    @pl.when(is_sample)
    def _():
        ys_ref[...] = out


def _post_out(rows, h, y, w_post, seq):
    d = h.shape[1]
    row = pl.BlockSpec((RB, d), lambda r: (r, 0))
    vec = pl.BlockSpec((1, d), lambda r: (0, 0))
    return pl.pallas_call(
        functools.partial(_post_out_kernel, rows=rows),
        grid=(rows.blocks,),
        in_specs=[row, row, vec],
        out_specs=[pl.BlockSpec((1, RB, d), rows.prompt_index),
                   pl.BlockSpec((RB, d), rows.sample_index)],
        out_shape=[jax.ShapeDtypeStruct((rows.nb, seq, d), F32),
                   jax.ShapeDtypeStruct((rows.sample_blocks * RB, d), F32)],
        compiler_params=_params(("arbitrary",)),
        name="post_out",
    )(h, y, w_post.reshape(1, d))


def _accumulate(acc_ref, x_ref, w_ref):
    k = pl.program_id(2)

    def prod():
        return jnp.dot(x_ref[...], w_ref[...].astype(BF16), preferred_element_type=F32)

    @pl.when(k == 0)
    def _():
        acc_ref[...] = prod()

    @pl.when(k > 0)
    def _():
        acc_ref[...] += prod()


def _mm_kernel(x_ref, w_ref, o_ref, *scratch, relu2):
    acc_ref = scratch[0] if scratch else o_ref
    _accumulate(acc_ref, x_ref, w_ref)

    if scratch:
        @pl.when(pl.program_id(2) == pl.num_programs(2) - 1)
        def _():
            acc = acc_ref[...]
            if relu2:
                acc = jnp.square(jnp.maximum(acc, 0.0))
            o_ref[...] = acc.astype(o_ref.dtype)


MM_TM = 2240
MM_TN = 1024
MM_TK = 1024


def _matmul(x, w, out_dtype, relu2=False, name="matmul", n=None, w_col=None):
    m, kd = x.shape
    n = w.shape[1] if n is None else n
    tm = _pick(m, MM_TM, 16)
    tn = _pick(n, MM_TN, LANES)
    tk = _pick(kd, MM_TK, LANES)
    w_col = (lambda j: j) if w_col is None else w_col
    assert out_dtype == BF16 or not relu2
    scratch = [pltpu.VMEM((tm, tn), F32)] if out_dtype != F32 else []
    return pl.pallas_call(
        functools.partial(_mm_kernel, relu2=relu2),
        grid=(m // tm, n // tn, kd // tk),
        in_specs=[pl.BlockSpec((tm, tk), lambda i, j, k: (i, k)),
                  pl.BlockSpec((tk, tn), lambda i, j, k: (k, w_col(j)))],
        out_specs=pl.BlockSpec((tm, tn), lambda i, j, k: (i, j)),
        out_shape=jax.ShapeDtypeStruct((m, n), out_dtype),
        scratch_shapes=scratch,
        compiler_params=_params(("parallel", "parallel", "arbitrary")),
        name=name,
    )(x, w)


def _merge_kernel(oa_ref, ob_ref, wa_ref, wb_ref, ga_ref, gb_ref, o_ref, acca_ref, accb_ref):
    _accumulate(acca_ref, oa_ref, wa_ref)
    _accumulate(accb_ref, ob_ref, wb_ref)

    @pl.when(pl.program_id(2) == pl.num_programs(2) - 1)
    def _():
        ga = jax.nn.sigmoid(ga_ref[...].astype(F32))
        gb = jax.nn.sigmoid(gb_ref[...].astype(F32))
        o_ref[...] = (ga * acca_ref[...] + gb * accb_ref[...]).astype(o_ref.dtype)


def _merge(oa, ob, wa, wb, z):
    m, kd = oa.shape
    n = wa.shape[1]
    tm = _pick(m, 1280, 16)
    tn = _pick(n, 1024, LANES)
    tk = _pick(kd, 512, LANES)
    ga0, gb0 = 0, n // tn
    return pl.pallas_call(
        _merge_kernel,
        grid=(m // tm, n // tn, kd // tk),
        in_specs=[pl.BlockSpec((tm, tk), lambda i, j, k: (i, k)),
                  pl.BlockSpec((tm, tk), lambda i, j, k: (i, k)),
                  pl.BlockSpec((tk, tn), lambda i, j, k: (k, j)),
                  pl.BlockSpec((tk, tn), lambda i, j, k: (k, j)),
                  pl.BlockSpec((tm, tn), lambda i, j, k: (i, ga0 + j)),
                  pl.BlockSpec((tm, tn), lambda i, j, k: (i, gb0 + j))],
        out_specs=pl.BlockSpec((tm, tn), lambda i, j, k: (i, j)),
        out_shape=jax.ShapeDtypeStruct((m, n), BF16),
        scratch_shapes=[pltpu.VMEM((tm, tn), F32), pltpu.VMEM((tm, tn), F32)],
        compiler_params=_params(("parallel", "parallel", "arbitrary")),
        name="branch_merge",
    )(oa, ob, wa, wb, z, z)


def _rope_tables(pos):
    half = ROT_DIM // 2
    inv_freq = ROPE_THETA ** (-jnp.arange(half, dtype=F32) * (2.0 / ROT_DIM))
    ang = pos.astype(F32)[:, None] * inv_freq[None, :]
    cos, sin = jnp.cos(ang), jnp.sin(ang)
    rows = pos.shape[0]
    rest = A_HEAD_DIM - ROT_DIM
    one = jnp.ones((rows, rest), F32)
    zero = jnp.zeros((rows, rest), F32)
    zh = jnp.zeros((rows, half), F32)
    reps = LANES // A_HEAD_DIM
    c = jnp.tile(jnp.concatenate([cos, cos, one], axis=1), (1, reps))
    s1 = jnp.tile(jnp.concatenate([-sin, zh, zero], axis=1), (1, reps))
    s2 = jnp.tile(jnp.concatenate([zh, sin, zero], axis=1), (1, reps))
    return c, s1, s2


def _rope_kernel(q_ref, k_ref, c_ref, s1_ref, s2_ref, qo_ref, ko_ref):
    half = ROT_DIM // 2
    c, s1, s2 = c_ref[...], s1_ref[...], s2_ref[...]

    def rot(x):
        return (x * c + pltpu.roll(x, LANES - half, 1) * s1 + pltpu.roll(x, half, 1) * s2)

    scale = A_HEAD_DIM ** -0.5
    for j in range(QA_DIM // LANES):
        sl = slice(j * LANES, (j + 1) * LANES)
        qo_ref[:, sl] = (rot(q_ref[:, sl].astype(F32)) * scale).astype(qo_ref.dtype)
    for j in range(KVA_DIM // LANES):
        sl = slice(j * LANES, (j + 1) * LANES)
        ko_ref[:, sl] = rot(k_ref[:, sl].astype(F32))


def _rope(z, tables):
    m = z.shape[0]
    tr = _pick(m, 256, 16)
    tab = pl.BlockSpec((tr, LANES), lambda i: (i, 0))
    return pl.pallas_call(
        _rope_kernel,
        grid=(m // tr,),
        in_specs=[pl.BlockSpec((tr, QA_DIM), lambda i: (i, COL_QA // QA_DIM)),
                  pl.BlockSpec((tr, KVA_DIM), lambda i: (i, COL_KA // KVA_DIM)),
                  tab, tab, tab],
        out_specs=[pl.BlockSpec((tr, QA_DIM), lambda i: (i, 0)),
                   pl.BlockSpec((tr, KVA_DIM), lambda i: (i, 0))],
        out_shape=[jax.ShapeDtypeStruct((m, QA_DIM), BF16),
                   jax.ShapeDtypeStruct((m, KVA_DIM), F32)],
        compiler_params=_params(("parallel",)),
        name="rope",
    )(z, z, *tables)


def _attend_heads(q_of, k_all, v_all, mask, sinks_ref, write, rows):
    mask_g = jnp.concatenate([mask] * A_GROUP, axis=0)

    def group(g):
        cs = slice(g * A_HEAD_DIM, (g + 1) * A_HEAD_DIM)
        kh, vh = k_all[:, cs], v_all[:, cs]
        qg = jnp.concatenate([q_of(g * A_GROUP + j) for j in range(A_GROUP)], axis=0)
        sk = jnp.concatenate(
            [jnp.full((rows, 1), sinks_ref[g * A_GROUP + j], F32) for j in range(A_GROUP)], axis=0)
        s = lax.dot_general(qg, kh, (((1,), (1,)), ((), ())), preferred_element_type=F32)
        yield
        s = jnp.where(mask_g, s, NEG)
        mx = jnp.maximum(jnp.max(s, axis=-1, keepdims=True), sk)
        p = jnp.exp(s - mx)
        den = jnp.sum(p, axis=-1, keepdims=True) + jnp.exp(sk - mx)
        o = jnp.dot(p.astype(BF16), vh, preferred_element_type=F32) / den
        yield
        for j in range(A_GROUP):
            write(g * A_GROUP + j, o[j * rows:(j + 1) * rows])

    return [group(g) for g in range(A_KV_HEADS)]


def _attn_prompt_kernel(sinks_ref, q_ref, kp_ref, kc_ref, vp_ref, vc_ref, o_ref):
    i = pl.program_id(1)
    blk = WINDOW
    k_all = jnp.concatenate([kp_ref[...], kc_ref[...]], axis=0).astype(BF16)
    v_all = jnp.concatenate([vp_ref[...], vc_ref[...]], axis=0).astype(BF16)
    qrow = i * blk + lax.broadcasted_iota(jnp.int32, (blk, 2 * blk), 0)
    krow = (i - 1) * blk + lax.broadcasted_iota(jnp.int32, (blk, 2 * blk), 1)
    diff = qrow - krow
    mask = (diff >= 0) & (diff <= WINDOW) & (krow >= FRONT)

    def q_of(h):
        return q_ref[:, h * A_HEAD_DIM:(h + 1) * A_HEAD_DIM]

    def write(h, o):
        o_ref[:, h * A_HEAD_DIM:(h + 1) * A_HEAD_DIM] = o.astype(o_ref.dtype)

    _interleave(_attend_heads(q_of, k_all, v_all, mask, sinks_ref, write, blk))


def _attn_prompt(qr, kr, z, sinks, nb, lp):
    blk = WINDOW
    nblk = lp // blk
    vcol = COL_VA // KVA_DIM

    def cur(b, i):
        return b * nblk + i

    def prev(b, i):
        return b * nblk + jnp.maximum(i - 1, 0)

    return pl.pallas_call(
        _attn_prompt_kernel,
        grid=(nb, nblk),
        in_specs=[pl.BlockSpec(memory_space=pltpu.SMEM),
                  pl.BlockSpec((blk, QA_DIM), lambda b, i: (cur(b, i), 0)),
                  pl.BlockSpec((blk, KVA_DIM), lambda b, i: (prev(b, i), 0)),
                  pl.BlockSpec((blk, KVA_DIM), lambda b, i: (cur(b, i), 0)),
                  pl.BlockSpec((blk, KVA_DIM), lambda b, i: (prev(b, i), vcol)),
                  pl.BlockSpec((blk, KVA_DIM), lambda b, i: (cur(b, i), vcol))],
        out_specs=pl.BlockSpec((blk, QA_DIM), lambda b, i: (cur(b, i), 0)),
        out_shape=jax.ShapeDtypeStruct((nb * lp, QA_DIM), BF16),
        compiler_params=_params(("parallel", "parallel")),
        name="attn_prompt",
    )(sinks, qr, kr, kr, z, z)


SAMPLE_ROWS = 8
SAMPLE_KEYS = 256


def _attn_sample_kernel(sinks_ref, q_ref, kn_ref, vn_ref, ck_ref, cv_ref, o_ref, ko_ref, vo_ref,
                        kall_ref, vall_ref, *, bb, n_buf, t_new):
    pad = SAMPLE_KEYS - n_buf - SAMPLE_ROWS
    qrow = lax.broadcasted_iota(jnp.int32, (SAMPLE_ROWS, SAMPLE_KEYS), 0)
    kcol = lax.broadcasted_iota(jnp.int32, (SAMPLE_ROWS, SAMPLE_KEYS), 1)
    diff = qrow + n_buf - kcol
    mask = (diff >= 0) & (diff <= WINDOW) & (kcol < n_buf + t_new)
    gens = []
    for b in range(bb):
        kall_ref[b, 0:n_buf, :] = ck_ref[b]
        kall_ref[b, n_buf:n_buf + SAMPLE_ROWS, :] = kn_ref[b]
        kall_ref[b, n_buf + SAMPLE_ROWS:, :] = jnp.zeros((pad, KVA_DIM), F32)
        vall_ref[b, 0:n_buf, :] = cv_ref[b]
        vall_ref[b, n_buf:n_buf + SAMPLE_ROWS, :] = vn_ref[b]
        vall_ref[b, n_buf + SAMPLE_ROWS:, :] = jnp.zeros((pad, KVA_DIM), F32)
        ko_ref[b] = kall_ref[b, t_new:t_new + n_buf, :]
        vo_ref[b] = vall_ref[b, t_new:t_new + n_buf, :]

        def q_of(h, b=b):
            return q_ref[b, :, h * A_HEAD_DIM:(h + 1) * A_HEAD_DIM].astype(BF16)

        def write(h, o, b=b):
            o_ref[b, :, h * A_HEAD_DIM:(h + 1) * A_HEAD_DIM] = o.astype(o_ref.dtype)

        gens += _attend_heads(q_of, kall_ref[b].astype(BF16), vall_ref[b].astype(BF16), mask,
                              sinks_ref, write, SAMPLE_ROWS)
    _interleave(gens)


def _attn_sample(qn, kn, vn, cache_k, cache_v, sinks, t_new):
    bd, n_buf, _ = cache_k.shape
    bb = _pick(bd, 4, 1)
    blk3 = lambda r, c: pl.BlockSpec((bb, r, c), lambda i: (i, 0, 0))
    return pl.pallas_call(
        functools.partial(_attn_sample_kernel, bb=bb, n_buf=n_buf, t_new=t_new),
        grid=(bd // bb,),
        in_specs=[pl.BlockSpec(memory_space=pltpu.SMEM),
                  blk3(SAMPLE_ROWS, QA_DIM), blk3(SAMPLE_ROWS, KVA_DIM), blk3(SAMPLE_ROWS, KVA_DIM),
                  blk3(n_buf, KVA_DIM), blk3(n_buf, KVA_DIM)],
        out_specs=[blk3(SAMPLE_ROWS, QA_DIM), blk3(n_buf, KVA_DIM), blk3(n_buf, KVA_DIM)],
        out_shape=[jax.ShapeDtypeStruct((bd, SAMPLE_ROWS, QA_DIM), BF16),
                   jax.ShapeDtypeStruct((bd, n_buf, KVA_DIM), F32),
                   jax.ShapeDtypeStruct((bd, n_buf, KVA_DIM), F32)],
        scratch_shapes=[pltpu.VMEM((bb, SAMPLE_KEYS, KVA_DIM), F32),
                        pltpu.VMEM((bb, SAMPLE_KEYS, KVA_DIM), F32)],
        compiler_params=_params(("parallel",)),
        name="attn_sample",
    )(sinks, qn, kn, vn, cache_k, cache_v)


TILE = 128
TOK = SUBLANES


def _softplus(x):
    return jnp.maximum(x, 0.0) + jnp.log(1.0 + jnp.exp(-jnp.abs(x)))


def _bf(x):
    return x.astype(BF16)


def _mm(a, b):
    return jnp.dot(_bf(a), _bf(b), preferred_element_type=F32)


def _mm_nt(a, b):
    return lax.dot_general(_bf(a), _bf(b), (((1,), (1,)), ((), ())), preferred_element_type=F32)


def _mm_tn(a, b):
    return lax.dot_general(a, b, (((0,), (0,)), ((), ())), preferred_element_type=F32)


def _tile_masks(block):
    r = lax.broadcasted_iota(jnp.int32, (TILE, TILE), 0)
    q = lax.broadcasted_iota(jnp.int32, (TILE, TILE), 1)

    def same(size):
        sh = size.bit_length() - 1
        return jnp.right_shift(r, sh) == jnp.right_shift(q, sh)

    tri = same(block) & (r >= q)
    strict = same(block) & (r > q)
    levels = []
    size = SUBLANES
    while size < block:
        levels.append(same(2 * size) & jnp.logical_not(same(size)))
        size *= 2
    return tri, strict, same(SUBLANES), levels


def _inverse_minus_eye(m, same8, levels):
    n1 = -jnp.where(same8, m, 0.0)
    n2 = _mm(n1, n1)
    yield
    n4 = _mm(n2, n2)
    n12 = _mm(n1, n2)
    yield
    a = n1 + n2 + n12
    d = a + n4 + _mm(a, n4)
    yield
    for mask in levels:
        off = jnp.where(mask, m, 0.0)
        p = off + _mm(off, d)
        yield
        d = d - (p + _mm(d, p))
        yield
    return d


def _delta_tile(q, k, v, gb, bb, grow, glast, masks):
    tri, strict, same8, levels = masks
    decay = jnp.where(tri, jnp.exp(jnp.where(tri, gb - grow, 0.0)), 0.0)
    eg = jnp.exp(gb)
    kb = k * bb
    aq = _mm_nt(jnp.concatenate([kb, q], axis=0), k)
    yield
    m = jnp.where(strict, aq[:TILE] * decay, 0.0)
    qk = aq[TILE:] * decay
    d = yield from _inverse_minus_eye(m, same8, levels)
    rhs = jnp.concatenate([v * bb, kb * eg], axis=1)
    sol = rhs + _mm(d, rhs)
    yield
    return sol[:, :B_HEAD_DIM], sol[:, B_HEAD_DIM:], qk, q * eg, k * jnp.exp(glast - gb)


def _interleave(gens):
    live = list(gens)
    while live:
        nxt = []
        for g in live:
            try:
                next(g)
                nxt.append(g)
            except StopIteration:
                pass
        live = nxt


def _l2(x):
    return x * lax.rsqrt(jnp.sum(x * x, axis=-1, keepdims=True) + EPS)


def _gated_norm(o, zb, dnorm):
    return _rms(o, dnorm) * (zb * jax.nn.sigmoid(zb))


def _delta_prompt_kernel(ab_ref, abt_ref, xq_ref, xk_ref, xv_ref, pq_ref, pk_ref, pv_ref, hist_ref,
                         zb_ref, s0_ref, convw_ref, arow_ref, acol_ref, dnorm_ref,
                         ob_ref, sfin_ref, s_ref, xx_ref, *, front, n_valid):
    c = pl.program_id(1)

    @pl.when(c == 0)
    def _():
        s_ref[...] = s0_ref[0]
        xx_ref[0:HIST_ROWS, :] = hist_ref[0]

    @pl.when(c > 0)
    def _():
        lo = PREV_ROWS - HIST_ROWS
        xx_ref[0:HIST_ROWS, 0:QB_DIM] = pq_ref[lo:, :].astype(F32)
        xx_ref[0:HIST_ROWS, QB_DIM:2 * QB_DIM] = pk_ref[lo:, :].astype(F32)
        xx_ref[0:HIST_ROWS, 2 * QB_DIM:] = pv_ref[lo:, :].astype(F32)

    xx_ref[HIST_ROWS:, 0:QB_DIM] = xq_ref[...].astype(F32)
    xx_ref[HIST_ROWS:, QB_DIM:2 * QB_DIM] = xk_ref[...].astype(F32)
    xx_ref[HIST_ROWS:, 2 * QB_DIM:] = xv_ref[...].astype(F32)

    pos_col = c * TILE + lax.broadcasted_iota(jnp.int32, (TILE, 1), 0)
    valid_col = ((pos_col >= front) & (pos_col < front + n_valid)).astype(F32)
    pos_row = c * TILE + lax.broadcasted_iota(jnp.int32, (1, TILE), 1)
    valid_row = ((pos_row >= front) & (pos_row < front + n_valid)).astype(F32)

    ab = ab_ref[...]
    beta = jax.nn.sigmoid(ab)
    g_col = -jnp.exp(arow_ref[0:1, :]) * _softplus(ab + arow_ref[1:2, :]) * valid_col
    g_row = -jnp.exp(acol_ref[:, 0:1]) * _softplus(abt_ref[0, 0] + acol_ref[:, 1:2]) * valid_row
    ri = lax.broadcasted_iota(jnp.int32, (TILE, TILE), 0)
    ci = lax.broadcasted_iota(jnp.int32, (TILE, TILE), 1)
    cum_col = jnp.dot((ri >= ci).astype(F32), g_col, precision=HI, preferred_element_type=F32)
    cum_row = jnp.dot(g_row, (ri <= ci).astype(F32), precision=HI, preferred_element_type=F32)
    masks = _tile_masks(TILE)
    dnorm = dnorm_ref[...]
    first = HIST_ROWS - (CONV_W - 1)

    def conv(col):
        cs = slice(col, col + B_HEAD_DIM)
        y = xx_ref[first:first + TILE, cs] * convw_ref[0:1, cs]
        for j in range(1, CONV_W):
            y = y + xx_ref[first + j:first + j + TILE, cs] * convw_ref[j:j + 1, cs]
        return y * jax.nn.sigmoid(y) * valid_col

    def head(h):
        off = h * B_HEAD_DIM
        q = _l2(conv(off)) * (B_HEAD_DIM ** -0.5)
        k = _l2(conv(QB_DIM + off))
        v = conv(2 * QB_DIM + off)
        gb = jnp.broadcast_to(cum_col[:, B_HEADS + h:B_HEADS + h + 1], (TILE, LANES))
        bb = jnp.broadcast_to(beta[:, h:h + 1], (TILE, LANES))
        grow = cum_row[B_HEADS + h:B_HEADS + h + 1, :]
        glast = gb[TILE - 1:TILE, :]
        u, w, qk, qd, kd = yield from _delta_tile(q, k, v, gb, bb, grow, glast, masks)
        s = s_ref[h]
        wq = _mm(jnp.concatenate([w, qd], axis=0), s)
        yield
        v_new = u - wq[:TILE]
        o = wq[TILE:] + _mm(qk, v_new)
        s_ref[h] = s * jnp.exp(glast) + _mm_tn(kd, v_new)
        yield
        zb = zb_ref[:, off:off + B_HEAD_DIM].astype(F32)
        ob_ref[:, off:off + B_HEAD_DIM] = _gated_norm(o, zb, dnorm).astype(ob_ref.dtype)

    _interleave(head(h) for h in range(B_HEADS))

    @pl.when(c == pl.num_programs(1) - 1)
    def _():
        sfin_ref[0] = s_ref[...]


def _delta_prompt(z, ab, abt, hist, s0, conv_w, arow, acol, dnorm, *, nseq, rows_per_seq, front,
                  n_valid):
    tps = rows_per_seq // TILE
    ppt = TILE // PREV_ROWS

    def cur(b, c):
        return b * tps + c

    def prev(b, c):
        return jnp.maximum((b * tps + c) * ppt - 1, 0)

    def zcol(j):
        return pl.BlockSpec((TILE, QB_DIM), lambda b, c: (cur(b, c), j))

    def pcol(j):
        return pl.BlockSpec((PREV_ROWS, QB_DIM), lambda b, c: (prev(b, c), j))

    whole = lambda a: pl.BlockSpec(a.shape, lambda b, c: (0,) * a.ndim)
    state = pl.BlockSpec((1, B_HEADS, B_HEAD_DIM, B_HEAD_DIM), lambda b, c: (b, 0, 0, 0))
    return pl.pallas_call(
        functools.partial(_delta_prompt_kernel, front=front, n_valid=n_valid),
        grid=(nseq, tps),
        in_specs=[pl.BlockSpec((TILE, LANES), lambda b, c: (cur(b, c), 0)),
                  pl.BlockSpec((1, 1, 2 * B_HEADS, TILE), lambda b, c: (b, c, 0, 0)),
                  zcol(0), zcol(1), zcol(2), pcol(0), pcol(1), pcol(2),
                  pl.BlockSpec((1, HIST_ROWS, CONV_DIM), lambda b, c: (b, 0, 0)),
                  zcol(COL_ZB // QB_DIM), state,
                  whole(conv_w), whole(arow), whole(acol), whole(dnorm)],
        out_specs=[pl.BlockSpec((TILE, QB_DIM), lambda b, c: (cur(b, c), 0)), state],
        out_shape=[jax.ShapeDtypeStruct((nseq * rows_per_seq, QB_DIM), BF16),
                   jax.ShapeDtypeStruct(s0.shape, F32)],
        scratch_shapes=[pltpu.VMEM((B_HEADS, B_HEAD_DIM, B_HEAD_DIM), F32),
                        pltpu.VMEM((HIST_ROWS + TILE, CONV_DIM), F32)],
        compiler_params=_params(("parallel", "arbitrary")),
        name="delta_prompt",
    )(ab, abt, z, z, z, z, z, z, hist, z, s0, conv_w, arow, acol, dnorm)


def _delta_sample_kernel(x_ref, ab_ref, abt_ref, hist_ref, s0_ref, convw_ref, arow_ref, alane_ref,
                         dnorm_ref, ob_ref, sout_ref, xx_ref, *, bb, t_new):
    masks = _tile_masks(TOK)
    trow = lax.broadcasted_iota(jnp.int32, (TOK, 1), 0)
    valid_t = (trow < t_new).astype(F32)
    tlane = jnp.bitwise_and(lax.broadcasted_iota(jnp.int32, (TOK, LANES), 1), TOK - 1)
    valid_lane = (tlane < t_new).astype(F32)
    dnorm = dnorm_ref[...]
    first = HIST_ROWS - (CONV_W - 1)

    def stack(a, base):
        return jnp.concatenate(
            [a[:, base + h * B_HEAD_DIM:base + (h + 1) * B_HEAD_DIM] for h in range(B_HEADS)], axis=0)

    def rows_of(col_of):
        return jnp.concatenate(
            [jnp.broadcast_to(col_of(h), (TOK, LANES)) for h in range(B_HEADS)], axis=0)

    def seq(b):
        xx_ref[0:HIST_ROWS, :] = hist_ref[b]
        xx_ref[HIST_ROWS:, :] = x_ref[b, :, 0:CONV_DIM]
        y = xx_ref[first:first + TOK, :] * convw_ref[0:1, :]
        for j in range(1, CONV_W):
            y = y + xx_ref[first + j:first + j + TOK, :] * convw_ref[j:j + 1, :]
        y = y * jax.nn.sigmoid(y) * valid_t
        q = _l2(stack(y, 0)) * (B_HEAD_DIM ** -0.5)
        k = _l2(stack(y, QB_DIM))
        v = stack(y, 2 * QB_DIM)

        ab = ab_ref[b]
        beta = jax.nn.sigmoid(ab)
        g = -jnp.exp(arow_ref[0:1, :]) * _softplus(ab + arow_ref[1:2, :]) * valid_t
        gl = (-jnp.exp(alane_ref[0:1, :]) * _softplus(abt_ref[b] + alane_ref[1:2, :])) * valid_lane
        step = 1
        while step < TOK:
            g = g + jnp.where(trow >= step, pltpu.roll(g, step, 0), 0.0)
            gl = gl + jnp.where(tlane >= step, pltpu.roll(gl, step, 1), 0.0)
            step *= 2
        gb = rows_of(lambda h: g[:, B_HEADS + h:B_HEADS + h + 1])
        bt = rows_of(lambda h: beta[:, h:h + 1])
        glast = rows_of(lambda h: g[TOK - 1:TOK, B_HEADS + h:B_HEADS + h + 1])
        u, w, qk, qd, kd = yield from _delta_tile(q, k, v, gb, bt, gl[0:1, :], glast, masks)

        wq = []
        for h in range(B_HEADS):
            rs = slice(h * TOK, (h + 1) * TOK)
            wq.append(_mm(jnp.concatenate([w[rs], qd[rs]], axis=0), s0_ref[b, h]))
        yield
        v_new = u - jnp.concatenate([x[:TOK] for x in wq], axis=0)
        o = jnp.concatenate([x[TOK:] for x in wq], axis=0) + _mm(qk, v_new)
        yield
        for h in range(B_HEADS):
            rs = slice(h * TOK, (h + 1) * TOK)
            sout_ref[b, h] = (s0_ref[b, h] * jnp.exp(glast[h * TOK:h * TOK + 1, :])
                              + _mm_tn(kd[rs], v_new[rs]))
        ob = _gated_norm(o, stack(x_ref[b], COL_ZB), dnorm)
        for h in range(B_HEADS):
            ob_ref[b, :, h * B_HEAD_DIM:(h + 1) * B_HEAD_DIM] = ob[h * TOK:(h + 1) * TOK]

    _interleave(seq(b) for b in range(bb))


def _delta_sample(xs, ab, abt, hist, s0, conv_w, arow, alane, dnorm, t_new):
    bd = xs.shape[0]
    bb = _pick(bd, 2, 1)
    blk = lambda a: pl.BlockSpec((bb,) + a.shape[1:], lambda i: (i,) + (0,) * (a.ndim - 1))
    whole = lambda a: pl.BlockSpec(a.shape, lambda i: (0,) * a.ndim)
    return pl.pallas_call(
        functools.partial(_delta_sample_kernel, bb=bb, t_new=t_new),
        grid=(bd // bb,),
        in_specs=[blk(xs), blk(ab), blk(abt), blk(hist), blk(s0),
                  whole(conv_w), whole(arow), whole(alane), whole(dnorm)],
        out_specs=[pl.BlockSpec((bb, TOK, QB_DIM), lambda i: (i, 0, 0)), blk(s0)],
        out_shape=[jax.ShapeDtypeStruct((bd, TOK, QB_DIM), F32),
                   jax.ShapeDtypeStruct(s0.shape, F32)],
        scratch_shapes=[pltpu.VMEM((HIST_ROWS + TOK, CONV_DIM), F32)],
        compiler_params=_params(("parallel",)),
        name="delta_sample",
    )(xs, ab, abt, hist, s0, conv_w, arow, alane, dnorm)


def _layer(rows, xp, meta, xs, tables, nb, lp, n_tok, bd, t_new, cache_k, cache_v, state_conv,
           state_delta, norm_mix_pre, norm_mix_post, norm_mlp_pre, norm_mlp_post, w_in, sinks, conv_w,
           a_log, dt_bias, delta_norm, w_branch_a, w_branch_b, w_out, w_up, w_down):
    p_rows = nb * lp

    assert SRC_CONV % MM_TN == 0 and COL_QA % MM_TN == 0 and SRC_BETA % LANES == 0
    n_front, src_front = COL_QA // MM_TN, SRC_CONV // MM_TN
    hn = _norm_in(rows, xp, meta, xs, norm_mix_pre)
    z = _matmul(hn, w_in, BF16, name="in_proj", n=Z_DIM,
                w_col=lambda j: jnp.where(j < n_front, j + src_front, j - n_front))
    ab = _matmul(hn, w_in, F32, name="gate_proj", n=LANES, w_col=lambda j: SRC_BETA // LANES)
    zg = _matmul(hn, w_in[:, SRC_GA:SRC_GA + 2 * D_MODEL].astype(BF16), BF16, name="gate_in_proj")

    qr, kr = _rope(z, tables)
    oa_p = _attn_prompt(qr, kr, z, sinks, nb, lp)

    def sample_pad(a):
        a = a[p_rows:].astype(F32).reshape(bd, t_new, a.shape[1])
        return jnp.pad(a, ((0, 0), (0, SAMPLE_ROWS - t_new), (0, 0)))

    n_buf = cache_k.shape[1]
    oa_s, k_s, v_s = _attn_sample(
        sample_pad(qr), sample_pad(kr), sample_pad(z[:, COL_VA:COL_VA + KVA_DIM]),
        cache_k.reshape(bd, n_buf, KVA_DIM), cache_v.reshape(bd, n_buf, KVA_DIM), sinks, t_new)
    oa = jnp.concatenate([oa_p, oa_s[:, :t_new].reshape(bd * t_new, QA_DIM)], axis=0)

    arow = jnp.zeros((2, LANES), F32)
    arow = arow.at[0, B_HEADS:2 * B_HEADS].set(a_log).at[1, B_HEADS:2 * B_HEADS].set(dt_bias)
    acol = jnp.zeros((2 * B_HEADS, LANES), F32)
    acol = acol.at[B_HEADS:, 0].set(a_log).at[B_HEADS:, 1].set(dt_bias)
    alane = jnp.stack([jnp.repeat(a_log, TOK), jnp.repeat(dt_bias, TOK)])
    dnorm = delta_norm.reshape(1, B_HEAD_DIM)
    abt_p = jnp.swapaxes(
        ab[:p_rows].reshape(nb, lp // TILE, TILE, LANES)[..., :2 * B_HEADS], 2, 3)
    ob_p, d_p = _delta_prompt(
        z, ab, abt_p, jnp.zeros((nb, HIST_ROWS, CONV_DIM), F32),
        jnp.zeros((nb, B_HEADS, B_HEAD_DIM, B_HEAD_DIM), F32),
        conv_w, arow, acol, dnorm, nseq=nb, rows_per_seq=lp, front=FRONT, n_valid=n_tok)

    x_tok = sample_pad(z[:, :COL_ZB + QB_DIM])
    ab_s = sample_pad(ab)
    abt_s = jnp.swapaxes(ab_s[..., B_HEADS:2 * B_HEADS], 1, 2).reshape(bd, 1, B_HEADS * TOK)
    hist_s = jnp.pad(state_conv, ((0, 0), (HIST_ROWS - (CONV_W - 1), 0), (0, 0)))
    ob_s, d_s = _delta_sample(x_tok, ab_s, abt_s, hist_s, state_delta, conv_w, arow, alane, dnorm, t_new)
    ob = jnp.concatenate(
        [ob_p, ob_s[:, :t_new].reshape(bd * t_new, QB_DIM).astype(BF16)], axis=0)

    merged = _merge(oa, ob, w_branch_a, w_branch_b, zg)
    y = _matmul(merged, w_out, F32, name="out_proj")
    h1, hn2 = _post_pre(rows, xp, meta, xs, y, norm_mix_post, norm_mlp_pre)
    u = _matmul(hn2, w_up, BF16, relu2=True, name="mlp_up")
    y2 = _matmul(u, w_down, F32, name="mlp_down")
    y_prompt, y_sample = _post_out(rows, h1, y2, norm_mlp_post, xp.shape[1])

    zp = z[:p_rows].reshape(nb, lp, Z_DIM)
    last = FRONT + n_tok
    k_p = kr[:p_rows].reshape(nb, lp, KVA_DIM)[:, last - WINDOW:last]
    v_p = zp[:, last - WINDOW:last, COL_VA:COL_VA + KVA_DIM].astype(F32)
    c_p = zp[:, last - (CONV_W - 1):last, :CONV_DIM].astype(F32)
    x_s = z[p_rows:, :CONV_DIM].astype(F32).reshape(bd, t_new, CONV_DIM)
    c_s = jnp.concatenate([state_conv, x_s], axis=1)[:, -(CONV_W - 1):]
    return y_prompt, y_sample, k_p, v_p, c_p, d_p, k_s, v_s, c_s, d_s


def kernel(x_prompt, x_sample, cache_win_k, cache_win_v, state_conv, state_delta, meta_tokens,
           norm_mix_pre, norm_mix_post, norm_mlp_pre, norm_mlp_post, w_in, sinks, conv_w,
           a_log, dt_bias, delta_norm, w_branch_a, w_branch_b, w_out, w_up, w_down):
    nb, seq, d = x_prompt.shape
    bd, t_new, _ = x_sample.shape
    depth = w_in.shape[0]
    assert depth == 1 and CONV_W - 1 <= t_new <= SAMPLE_ROWS == TOK
    n_tok = N_META + seq
    assert n_tok >= WINDOW
    lp = -(-(FRONT + n_tok) // WINDOW) * WINDOW
    rows = _Rows(nb, seq, lp, bd * t_new)

    pos_p = jnp.maximum(jnp.arange(lp) - FRONT, 0)
    pos = jnp.concatenate([jnp.tile(pos_p, nb), jnp.tile(PAST_LEN + jnp.arange(t_new), bd)])
    tables = _rope_tables(pos)

    l = 0
    y_prompt, y_sample, k_p, v_p, c_p, d_p, k_s, v_s, c_s, d_s = _layer(
        rows, x_prompt, meta_tokens, x_sample.reshape(bd * t_new, d), tables, nb, lp, n_tok, bd,
        t_new, cache_win_k[l], cache_win_v[l], state_conv[l],
        state_delta[l], norm_mix_pre[l], norm_mix_post[l], norm_mlp_pre[l], norm_mlp_post[l],
        w_in[l], sinks[l], conv_w[l], a_log[l], dt_bias[l], delta_norm[l], w_branch_a[l],
        w_branch_b[l], w_out[l], w_up[l], w_down[l])

    y_sample = y_sample.reshape(bd, t_new, d)
    n_buf = cache_win_k.shape[2]
    kv = lambda a, n, r: a.reshape(1, n, r, A_KV_HEADS, A_HEAD_DIM)
    return (y_prompt, y_sample,
            kv(k_p, nb, WINDOW), kv(v_p, nb, WINDOW), c_p[None], d_p[None],
            kv(k_s, bd, n_buf), kv(v_s, bd, n_buf), c_s[None], d_s[None])
```

```python
import functools

import jax
import jax.numpy as jnp
from jax import lax
from jax.experimental import pallas as pl
from jax.experimental.pallas import tpu as pltpu

D_MODEL = 4096
PAST_LEN = 8192
N_META = 16
WINDOW = 128
A_HEADS = 32
A_KV_HEADS = 8
A_GROUP = A_HEADS // A_KV_HEADS
A_HEAD_DIM = 64
ROT_DIM = A_HEAD_DIM // 4
ROPE_THETA = 500000.0
B_HEADS = 16
B_HEAD_DIM = 128
CONV_W = 4
CHUNK = 64
D_FF = 4 * D_MODEL
EPS = 1e-6

QA_DIM = A_HEADS * A_HEAD_DIM
KVA_DIM = A_KV_HEADS * A_HEAD_DIM
QB_DIM = B_HEADS * B_HEAD_DIM
CONV_DIM = 3 * QB_DIM
FRONT = (-N_META) % CHUNK

COL_CONV = 0
COL_ZB = COL_CONV + CONV_DIM
COL_QA = COL_ZB + QB_DIM
COL_KA = COL_QA + QA_DIM
COL_VA = COL_KA + KVA_DIM
Z_DIM = COL_VA + KVA_DIM
SRC_QA = 0
SRC_KA = SRC_QA + QA_DIM
SRC_VA = SRC_KA + KVA_DIM
SRC_CONV = SRC_VA + KVA_DIM
SRC_ZB = SRC_CONV + CONV_DIM
SRC_BETA = SRC_ZB + QB_DIM
SRC_ALPHA = SRC_BETA + B_HEADS
SRC_GA = SRC_ALPHA + B_HEADS
SRC_GB = SRC_GA + D_MODEL
LANES = 128
SUBLANES = 8
RB = FRONT + N_META
HIST_ROWS = 8
PREV_ROWS = 16
NEG = -1e30
VMEM_LIMIT = 56 * 1024 * 1024
HI = lax.Precision.HIGHEST
F32 = jnp.float32
BF16 = jnp.bfloat16


def _pick(n, target, align):
    best = None
    for t in range(align, min(n, target) + 1, align):
        if n % t == 0:
            best = t
    return best if best is not None else n


def _params(sem):
    return pltpu.CompilerParams(dimension_semantics=sem, vmem_limit_bytes=VMEM_LIMIT)


def _rms(x, w):
    return x * lax.rsqrt(jnp.mean(x * x, axis=-1, keepdims=True) + EPS) * w


class _Rows:
    def __init__(self, nb, seq, lp, n_sample):
        assert FRONT + N_META == RB and seq % RB == 0 and lp % RB == 0 and n_sample % RB == 0
        self.nb, self.bps, self.seq_blocks = nb, lp // RB, seq // RB
        self.sample_blocks = n_sample // RB
        self.blocks = nb * self.bps + self.sample_blocks

    def prompt_index(self, r):
        b = jnp.minimum(r // self.bps, self.nb - 1)
        s = jnp.where(r < self.nb * self.bps, jnp.clip(r % self.bps - 1, 0, self.seq_blocks - 1),
                      self.seq_blocks - 1)
        return b, s, 0

    def sample_index(self, r):
        return jnp.clip(r - self.nb * self.bps, 0, self.sample_blocks - 1), 0

    def kinds(self, r):
        rb = r % self.bps
        is_p = r < self.nb * self.bps
        return (is_p & (rb == 0), is_p & (rb >= 1) & (rb <= self.seq_blocks),
                is_p & (rb > self.seq_blocks), jnp.logical_not(is_p))

    def in_specs(self, d):
        return [pl.BlockSpec((1, RB, d), self.prompt_index),
                pl.BlockSpec((N_META, d), lambda r: (0, 0)),
                pl.BlockSpec((RB, d), self.sample_index)]

    def gather(self, r, xp_ref, meta_ref, xs_ref, h_ref):
        is_meta, is_seq, is_zero, is_sample = self.kinds(r)

        @pl.when(is_meta)
        def _():
            h_ref[0:FRONT, :] = jnp.zeros((FRONT, h_ref.shape[1]), F32)
            h_ref[FRONT:, :] = meta_ref[...]

        @pl.when(is_seq)
        def _():
            h_ref[...] = xp_ref[0]

        @pl.when(is_zero)
        def _():
            h_ref[...] = jnp.zeros_like(h_ref)

        @pl.when(is_sample)
        def _():
            h_ref[...] = xs_ref[...]


def _norm_in_kernel(xp_ref, meta_ref, xs_ref, w_ref, o_ref, h_ref, *, rows):
    rows.gather(pl.program_id(0), xp_ref, meta_ref, xs_ref, h_ref)
    o_ref[...] = _rms(h_ref[...], w_ref[...]).astype(o_ref.dtype)


def _norm_in(rows, xp, meta, xs, w):
    d = xp.shape[-1]
    row = pl.BlockSpec((RB, d), lambda r: (r, 0))
    vec = pl.BlockSpec((1, d), lambda r: (0, 0))
    return pl.pallas_call(
        functools.partial(_norm_in_kernel, rows=rows),
        grid=(rows.blocks,),
        in_specs=rows.in_specs(d) + [vec],
        out_specs=row,
        out_shape=jax.ShapeDtypeStruct((rows.blocks * RB, d), BF16),
        scratch_shapes=[pltpu.VMEM((RB, d), F32)],
        compiler_params=_params(("parallel",)),
        name="norm_in",
    )(xp, meta, xs, w.reshape(1, d))


def _post_pre_kernel(xp_ref, meta_ref, xs_ref, y_ref, wpost_ref, wpre_ref, h1_ref, hn_ref, h_ref,
                     *, rows):
    rows.gather(pl.program_id(0), xp_ref, meta_ref, xs_ref, h_ref)
    h1 = h_ref[...] + _rms(y_ref[...], wpost_ref[...])
    h1_ref[...] = h1
    hn_ref[...] = _rms(h1, wpre_ref[...]).astype(hn_ref.dtype)


def _post_pre(rows, xp, meta, xs, y, w_post, w_pre):
    d = xp.shape[-1]
    m = rows.blocks * RB
    row = pl.BlockSpec((RB, d), lambda r: (r, 0))
    vec = pl.BlockSpec((1, d), lambda r: (0, 0))
    return pl.pallas_call(
        functools.partial(_post_pre_kernel, rows=rows),
        grid=(rows.blocks,),
        in_specs=rows.in_specs(d) + [row, vec, vec],
        out_specs=[row, row],
        out_shape=[jax.ShapeDtypeStruct((m, d), F32), jax.ShapeDtypeStruct((m, d), BF16)],
        scratch_shapes=[pltpu.VMEM((RB, d), F32)],
        compiler_params=_params(("parallel",)),
        name="post_pre_norm",
    )(xp, meta, xs, y, w_post.reshape(1, d), w_pre.reshape(1, d))


def _post_out_kernel(h_ref, y_ref, wpost_ref, yp_ref, ys_ref, *, rows):
    _, is_seq, _, is_sample = rows.kinds(pl.program_id(0))
    out = h_ref[...] + _rms(y_ref[...], wpost_ref[...])

    @pl.when(is_seq)
    def _():
        yp_ref[0] = out

    @pl.when(is_sample)
    def _():
        ys_ref[...] = out


def _post_out(rows, h, y, w_post, seq):
    d = h.shape[1]
    row = pl.BlockSpec((RB, d), lambda r: (r, 0))
    vec = pl.BlockSpec((1, d), lambda r: (0, 0))
    return pl.pallas_call(
        functools.partial(_post_out_kernel, rows=rows),
        grid=(rows.blocks,),
        in_specs=[row, row, vec],
        out_specs=[pl.BlockSpec((1, RB, d), rows.prompt_index),
                   pl.BlockSpec((RB, d), rows.sample_index)],
        out_shape=[jax.ShapeDtypeStruct((rows.nb, seq, d), F32),
                   jax.ShapeDtypeStruct((rows.sample_blocks * RB, d), F32)],
        compiler_params=_params(("arbitrary",)),
        name="post_out",
    )(h, y, w_post.reshape(1, d))


def _accumulate(acc_ref, x_ref, w_ref):
    k = pl.program_id(2)

    def prod():
        return jnp.dot(x_ref[...], w_ref[...].astype(BF16), preferred_element_type=F32)

    @pl.when(k == 0)
    def _():
        acc_ref[...] = prod()

    @pl.when(k > 0)
    def _():
        acc_ref[...] += prod()


def _mm_kernel(x_ref, w_ref, o_ref, *scratch, relu2):
    acc_ref = scratch[0] if scratch else o_ref
    _accumulate(acc_ref, x_ref, w_ref)

    if scratch:
        @pl.when(pl.program_id(2) == pl.num_programs(2) - 1)
        def _():
            acc = acc_ref[...]
            if relu2:
                acc = jnp.square(jnp.maximum(acc, 0.0))
            o_ref[...] = acc.astype(o_ref.dtype)


MM_TM = 1792
MM_TN = 1024
MM_TK = 2048


def _matmul(x, w, out_dtype, relu2=False, name="matmul", n=None, w_col=None):
    m, kd = x.shape
    n = w.shape[1] if n is None else n
    tm = _pick(m, MM_TM, 16)
    tn = _pick(n, MM_TN, LANES)
    tk = _pick(kd, MM_TK, LANES)
    w_col = (lambda j: j) if w_col is None else w_col
    assert out_dtype == BF16 or not relu2
    scratch = [pltpu.VMEM((tm, tn), F32)] if out_dtype != F32 else []
    return pl.pallas_call(
        functools.partial(_mm_kernel, relu2=relu2),
        grid=(m // tm, n // tn, kd // tk),
        in_specs=[pl.BlockSpec((tm, tk), lambda i, j, k: (i, k)),
                  pl.BlockSpec((tk, tn), lambda i, j, k: (k, w_col(j)))],
        out_specs=pl.BlockSpec((tm, tn), lambda i, j, k: (i, j)),
        out_shape=jax.ShapeDtypeStruct((m, n), out_dtype),
        scratch_shapes=scratch,
        compiler_params=_params(("parallel", "parallel", "arbitrary")),
        name=name,
    )(x, w)


def _merge_kernel(oa_ref, ob_ref, wa_ref, wb_ref, ga_ref, gb_ref, o_ref, acc_ref):
    def gated(g_ref, x_ref, w_ref):
        prod = jnp.dot(x_ref[...], w_ref[...].astype(BF16), preferred_element_type=F32)
        return jax.nn.sigmoid(g_ref[...].astype(F32)) * prod

    acc_ref[...] = gated(ga_ref, oa_ref, wa_ref)
    o_ref[...] = (acc_ref[...] + gated(gb_ref, ob_ref, wb_ref)).astype(o_ref.dtype)


def _merge(oa, ob, wa, wb, z):
    m, kd = oa.shape
    n = wa.shape[1]
    tm = _pick(m, 1280, 16)
    tn = _pick(n, 512, LANES)
    gb0 = n // tn
    act = pl.BlockSpec((tm, kd), lambda i, j: (i, 0))
    wgt = pl.BlockSpec((kd, tn), lambda i, j: (0, j))
    return pl.pallas_call(
        _merge_kernel,
        grid=(m // tm, n // tn),
        in_specs=[act, act, wgt, wgt,
                  pl.BlockSpec((tm, tn), lambda i, j: (i, j)),
                  pl.BlockSpec((tm, tn), lambda i, j: (i, gb0 + j))],
        out_specs=pl.BlockSpec((tm, tn), lambda i, j: (i, j)),
        out_shape=jax.ShapeDtypeStruct((m, n), BF16),
        scratch_shapes=[pltpu.VMEM((tm, tn), F32)],
        compiler_params=_params(("parallel", "arbitrary")),
        name="branch_merge",
    )(oa, ob, wa, wb, z, z)


def _repack_kernel(a_ref, b_ref, o_ref, *, shift):
    n = a_ref.shape[1]
    body = pltpu.roll(a_ref[...], n - shift, 1)
    tail = pltpu.roll(b_ref[...], LANES - shift, 1)
    lane = lax.broadcasted_iota(jnp.int32, tail.shape, 1)
    o_ref[:, 0:n - LANES] = body[:, 0:n - LANES].astype(o_ref.dtype)
    o_ref[:, n - LANES:] = jnp.where(lane < LANES - shift, body[:, n - LANES:], tail).astype(o_ref.dtype)


def _repack_gate_weights(w_in):
    kd = w_in.shape[0]
    shift = SRC_GA % LANES
    n = 2 * D_MODEL
    tn = MM_TN
    tk = _pick(kd, 512, SUBLANES)
    assert (SRC_GA - shift) % tn == 0 and n % tn == 0 and 0 < shift
    c0 = (SRC_GA - shift) // tn
    return pl.pallas_call(
        functools.partial(_repack_kernel, shift=shift),
        grid=(kd // tk, n // tn),
        in_specs=[pl.BlockSpec((tk, tn), lambda k, j: (k, c0 + j)),
                  pl.BlockSpec((tk, LANES), lambda k, j: (k, (c0 + j + 1) * (tn // LANES)))],
        out_specs=pl.BlockSpec((tk, tn), lambda k, j: (k, j)),
        out_shape=jax.ShapeDtypeStruct((kd, n), BF16),
        compiler_params=_params(("parallel", "parallel")),
        name="repack_gate_weights",
    )(w_in, w_in)


def _rope_tables(pos):
    half = ROT_DIM // 2
    inv_freq = ROPE_THETA ** (-jnp.arange(half, dtype=F32) * (2.0 / ROT_DIM))
    ang = pos.astype(F32)[:, None] * inv_freq[None, :]
    cos, sin = jnp.cos(ang), jnp.sin(ang)
    rows = pos.shape[0]
    rest = A_HEAD_DIM - ROT_DIM
    one = jnp.ones((rows, rest), F32)
    zero = jnp.zeros((rows, rest), F32)
    zh = jnp.zeros((rows, half), F32)
    reps = LANES // A_HEAD_DIM
    c = jnp.tile(jnp.concatenate([cos, cos, one], axis=1), (1, reps))
    s1 = jnp.tile(jnp.concatenate([-sin, zh, zero], axis=1), (1, reps))
    s2 = jnp.tile(jnp.concatenate([zh, sin, zero], axis=1), (1, reps))
    return c, s1, s2


def _rope_kernel(q_ref, k_ref, c_ref, s1_ref, s2_ref, qo_ref, ko_ref):
    half = ROT_DIM // 2
    c, s1, s2 = c_ref[...], s1_ref[...], s2_ref[...]

    def rot(x):
        return (x * c + pltpu.roll(x, LANES - half, 1) * s1 + pltpu.roll(x, half, 1) * s2)

    scale = A_HEAD_DIM ** -0.5
    for j in range(QA_DIM // LANES):
        sl = slice(j * LANES, (j + 1) * LANES)
        qo_ref[:, sl] = (rot(q_ref[:, sl].astype(F32)) * scale).astype(qo_ref.dtype)
    for j in range(KVA_DIM // LANES):
        sl = slice(j * LANES, (j + 1) * LANES)
        ko_ref[:, sl] = rot(k_ref[:, sl].astype(F32))


def _rope(z, tables):
    m = z.shape[0]
    tr = _pick(m, 256, 16)
    tab = pl.BlockSpec((tr, LANES), lambda i: (i, 0))
    return pl.pallas_call(
        _rope_kernel,
        grid=(m // tr,),
        in_specs=[pl.BlockSpec((tr, QA_DIM), lambda i: (i, COL_QA // QA_DIM)),
                  pl.BlockSpec((tr, KVA_DIM), lambda i: (i, COL_KA // KVA_DIM)),
                  tab, tab, tab],
        out_specs=[pl.BlockSpec((tr, QA_DIM), lambda i: (i, 0)),
                   pl.BlockSpec((tr, KVA_DIM), lambda i: (i, 0))],
        out_shape=[jax.ShapeDtypeStruct((m, QA_DIM), BF16),
                   jax.ShapeDtypeStruct((m, KVA_DIM), F32)],
        compiler_params=_params(("parallel",)),
        name="rope",
    )(z, z, *tables)


def _attend_heads(q_of, k_all, v_all, mask, sinks_ref, write, rows):
    mask_g = jnp.concatenate([mask] * A_GROUP, axis=0)

    def group(g):
        cs = slice(g * A_HEAD_DIM, (g + 1) * A_HEAD_DIM)
        kh, vh = k_all[:, cs], v_all[:, cs]
        qg = jnp.concatenate([q_of(g * A_GROUP + j) for j in range(A_GROUP)], axis=0)
        sk = jnp.concatenate(
            [jnp.full((rows, 1), sinks_ref[g * A_GROUP + j], F32) for j in range(A_GROUP)], axis=0)
        s = lax.dot_general(qg, kh, (((1,), (1,)), ((), ())), preferred_element_type=F32)
        yield
        s = jnp.where(mask_g, s, NEG)
        mx = jnp.maximum(jnp.max(s, axis=-1, keepdims=True), sk)
        p = jnp.exp(s - mx)
        den = jnp.sum(p, axis=-1, keepdims=True) + jnp.exp(sk - mx)
        o = jnp.dot(p.astype(BF16), vh, preferred_element_type=F32) / den
        yield
        for j in range(A_GROUP):
            write(g * A_GROUP + j, o[j * rows:(j + 1) * rows])

    return [group(g) for g in range(A_KV_HEADS)]


def _attn_prompt_kernel(sinks_ref, q_ref, kp_ref, kc_ref, vp_ref, vc_ref, o_ref):
    i = pl.program_id(1)
    blk = WINDOW
    k_all = jnp.concatenate([kp_ref[...], kc_ref[...]], axis=0).astype(BF16)
    v_all = jnp.concatenate([vp_ref[...], vc_ref[...]], axis=0).astype(BF16)
    qrow = i * blk + lax.broadcasted_iota(jnp.int32, (blk, 2 * blk), 0)
    krow = (i - 1) * blk + lax.broadcasted_iota(jnp.int32, (blk, 2 * blk), 1)
    diff = qrow - krow
    mask = (diff >= 0) & (diff <= WINDOW) & (krow >= FRONT)

    def q_of(h):
        return q_ref[:, h * A_HEAD_DIM:(h + 1) * A_HEAD_DIM]

    def write(h, o):
        o_ref[:, h * A_HEAD_DIM:(h + 1) * A_HEAD_DIM] = o.astype(o_ref.dtype)

    _interleave(_attend_heads(q_of, k_all, v_all, mask, sinks_ref, write, blk))


def _attn_prompt(qr, kr, z, sinks, nb, lp):
    blk = WINDOW
    nblk = lp // blk
    vcol = COL_VA // KVA_DIM

    def cur(b, i):
        return b * nblk + i

    def prev(b, i):
        return b * nblk + jnp.maximum(i - 1, 0)

    return pl.pallas_call(
        _attn_prompt_kernel,
        grid=(nb, nblk),
        in_specs=[pl.BlockSpec(memory_space=pltpu.SMEM),
                  pl.BlockSpec((blk, QA_DIM), lambda b, i: (cur(b, i), 0)),
                  pl.BlockSpec((blk, KVA_DIM), lambda b, i: (prev(b, i), 0)),
                  pl.BlockSpec((blk, KVA_DIM), lambda b, i: (cur(b, i), 0)),
                  pl.BlockSpec((blk, KVA_DIM), lambda b, i: (prev(b, i), vcol)),
                  pl.BlockSpec((blk, KVA_DIM), lambda b, i: (cur(b, i), vcol))],
        out_specs=pl.BlockSpec((blk, QA_DIM), lambda b, i: (cur(b, i), 0)),
        out_shape=jax.ShapeDtypeStruct((nb * lp, QA_DIM), BF16),
        compiler_params=_params(("parallel", "parallel")),
        name="attn_prompt",
    )(sinks, qr, kr, kr, z, z)


SAMPLE_ROWS = 8
SAMPLE_KEYS = 256


def _attn_sample_kernel(sinks_ref, q_ref, kn_ref, vn_ref, ck_ref, cv_ref, o_ref, ko_ref, vo_ref,
                        kall_ref, vall_ref, *, bb, n_buf, t_new):
    pad = SAMPLE_KEYS - n_buf - SAMPLE_ROWS
    qrow = lax.broadcasted_iota(jnp.int32, (SAMPLE_ROWS, SAMPLE_KEYS), 0)
    kcol = lax.broadcasted_iota(jnp.int32, (SAMPLE_ROWS, SAMPLE_KEYS), 1)
    diff = qrow + n_buf - kcol
    mask = (diff >= 0) & (diff <= WINDOW) & (kcol < n_buf + t_new)
    gens = []
    for b in range(bb):
        kall_ref[b, 0:n_buf, :] = ck_ref[b]
        kall_ref[b, n_buf:n_buf + SAMPLE_ROWS, :] = kn_ref[b]
        kall_ref[b, n_buf + SAMPLE_ROWS:, :] = jnp.zeros((pad, KVA_DIM), F32)
        vall_ref[b, 0:n_buf, :] = cv_ref[b]
        vall_ref[b, n_buf:n_buf + SAMPLE_ROWS, :] = vn_ref[b]
        vall_ref[b, n_buf + SAMPLE_ROWS:, :] = jnp.zeros((pad, KVA_DIM), F32)
        ko_ref[b] = kall_ref[b, t_new:t_new + n_buf, :]
        vo_ref[b] = vall_ref[b, t_new:t_new + n_buf, :]

        def q_of(h, b=b):
            return q_ref[b, :, h * A_HEAD_DIM:(h + 1) * A_HEAD_DIM].astype(BF16)

        def write(h, o, b=b):
            o_ref[b, :, h * A_HEAD_DIM:(h + 1) * A_HEAD_DIM] = o.astype(o_ref.dtype)

        gens += _attend_heads(q_of, kall_ref[b].astype(BF16), vall_ref[b].astype(BF16), mask,
                              sinks_ref, write, SAMPLE_ROWS)
    _interleave(gens)


def _attn_sample(qn, kn, vn, cache_k, cache_v, sinks, t_new):
    bd, n_buf, _ = cache_k.shape
    bb = _pick(bd, 8, 1)
    blk3 = lambda r, c: pl.BlockSpec((bb, r, c), lambda i: (i, 0, 0))
    return pl.pallas_call(
        functools.partial(_attn_sample_kernel, bb=bb, n_buf=n_buf, t_new=t_new),
        grid=(bd // bb,),
        in_specs=[pl.BlockSpec(memory_space=pltpu.SMEM),
                  blk3(SAMPLE_ROWS, QA_DIM), blk3(SAMPLE_ROWS, KVA_DIM), blk3(SAMPLE_ROWS, KVA_DIM),
                  blk3(n_buf, KVA_DIM), blk3(n_buf, KVA_DIM)],
        out_specs=[blk3(SAMPLE_ROWS, QA_DIM), blk3(n_buf, KVA_DIM), blk3(n_buf, KVA_DIM)],
        out_shape=[jax.ShapeDtypeStruct((bd, SAMPLE_ROWS, QA_DIM), BF16),
                   jax.ShapeDtypeStruct((bd, n_buf, KVA_DIM), F32),
                   jax.ShapeDtypeStruct((bd, n_buf, KVA_DIM), F32)],
        scratch_shapes=[pltpu.VMEM((bb, SAMPLE_KEYS, KVA_DIM), F32),
                        pltpu.VMEM((bb, SAMPLE_KEYS, KVA_DIM), F32)],
        compiler_params=_params(("parallel",)),
        name="attn_sample",
    )(sinks, qn, kn, vn, cache_k, cache_v)


TILE = 128
TOK = SUBLANES


def _softplus(x):
    return jnp.maximum(x, 0.0) + jnp.log(1.0 + jnp.exp(-jnp.abs(x)))


def _bf(x):
    return x.astype(BF16)


def _mm(a, b):
    return jnp.dot(_bf(a), _bf(b), preferred_element_type=F32)


def _mm_nt(a, b):
    return lax.dot_general(_bf(a), _bf(b), (((1,), (1,)), ((), ())), preferred_element_type=F32)


def _mm_tn(a, b):
    return lax.dot_general(a, b, (((0,), (0,)), ((), ())), preferred_element_type=F32)


def _tile_masks(block):
    r = lax.broadcasted_iota(jnp.int32, (TILE, TILE), 0)
    q = lax.broadcasted_iota(jnp.int32, (TILE, TILE), 1)

    def same(size):
        sh = size.bit_length() - 1
        return jnp.right_shift(r, sh) == jnp.right_shift(q, sh)

    tri = same(block) & (r >= q)
    strict = same(block) & (r > q)
    levels = []
    size = SUBLANES
    while size < block:
        levels.append(same(2 * size) & jnp.logical_not(same(size)))
        size *= 2
    return tri, strict, same(SUBLANES), levels


def _inverse_minus_eye(m, same8, levels):
    n1 = -jnp.where(same8, m, 0.0)
    n2 = _mm(n1, n1)
    yield
    n4 = _mm(n2, n2)
    n12 = _mm(n1, n2)
    yield
    a = n1 + n2 + n12
    d = a + n4 + _mm(a, n4)
    yield
    for mask in levels:
        off = jnp.where(mask, m, 0.0)
        p = off + _mm(off, d)
        yield
        d = d - (p + _mm(d, p))
        yield
    return d


def _delta_tile(q, k, v, gb, bb, grow, glast, masks):
    tri, strict, same8, levels = masks
    decay = jnp.where(tri, jnp.exp(jnp.where(tri, gb - grow, 0.0)), 0.0)
    eg = jnp.exp(gb)
    kb = k * bb
    aq = _mm_nt(jnp.concatenate([kb, q], axis=0), k)
    yield
    m = jnp.where(strict, aq[:TILE] * decay, 0.0)
    qk = aq[TILE:] * decay
    d = yield from _inverse_minus_eye(m, same8, levels)
    rhs = jnp.concatenate([v * bb, kb * eg], axis=1)
    sol = rhs + _mm(d, rhs)
    yield
    return sol[:, :B_HEAD_DIM], sol[:, B_HEAD_DIM:], qk, q * eg, k * jnp.exp(glast - gb)


def _interleave(gens):
    live = list(gens)
    while live:
        nxt = []
        for g in live:
            try:
                next(g)
                nxt.append(g)
            except StopIteration:
                pass
        live = nxt


def _l2(x):
    return x * lax.rsqrt(jnp.sum(x * x, axis=-1, keepdims=True) + EPS)


def _gated_norm(o, zb, dnorm):
    return _rms(o, dnorm) * (zb * jax.nn.sigmoid(zb))


def _delta_prompt_kernel(ab_ref, abt_ref, xq_ref, xk_ref, xv_ref, pq_ref, pk_ref, pv_ref, hist_ref,
                         zb_ref, s0_ref, convw_ref, arow_ref, acol_ref, dnorm_ref,
                         ob_ref, sfin_ref, s_ref, xx_ref, *, front, n_valid):
    c = pl.program_id(1)

    @pl.when(c == 0)
    def _():
        s_ref[...] = s0_ref[0]
        xx_ref[0:HIST_ROWS, :] = hist_ref[0]

    @pl.when(c > 0)
    def _():
        lo = PREV_ROWS - HIST_ROWS
        xx_ref[0:HIST_ROWS, 0:QB_DIM] = pq_ref[lo:, :].astype(F32)
        xx_ref[0:HIST_ROWS, QB_DIM:2 * QB_DIM] = pk_ref[lo:, :].astype(F32)
        xx_ref[0:HIST_ROWS, 2 * QB_DIM:] = pv_ref[lo:, :].astype(F32)

    xx_ref[HIST_ROWS:, 0:QB_DIM] = xq_ref[...].astype(F32)
    xx_ref[HIST_ROWS:, QB_DIM:2 * QB_DIM] = xk_ref[...].astype(F32)
    xx_ref[HIST_ROWS:, 2 * QB_DIM:] = xv_ref[...].astype(F32)

    pos_col = c * TILE + lax.broadcasted_iota(jnp.int32, (TILE, 1), 0)
    valid_col = ((pos_col >= front) & (pos_col < front + n_valid)).astype(F32)
    pos_row = c * TILE + lax.broadcasted_iota(jnp.int32, (1, TILE), 1)
    valid_row = ((pos_row >= front) & (pos_row < front + n_valid)).astype(F32)

    ab = ab_ref[...]
    beta = jax.nn.sigmoid(ab)
    g_col = -jnp.exp(arow_ref[0:1, :]) * _softplus(ab + arow_ref[1:2, :]) * valid_col
    g_row = -jnp.exp(acol_ref[:, 0:1]) * _softplus(abt_ref[0, 0] + acol_ref[:, 1:2]) * valid_row
    ri = lax.broadcasted_iota(jnp.int32, (TILE, TILE), 0)
    ci = lax.broadcasted_iota(jnp.int32, (TILE, TILE), 1)
    cum_col = jnp.dot((ri >= ci).astype(F32), g_col, precision=HI, preferred_element_type=F32)
    cum_row = jnp.dot(g_row, (ri <= ci).astype(F32), precision=HI, preferred_element_type=F32)
    masks = _tile_masks(TILE)
    dnorm = dnorm_ref[...]
    first = HIST_ROWS - (CONV_W - 1)

    def conv(col):
        cs = slice(col, col + B_HEAD_DIM)
        y = xx_ref[first:first + TILE, cs] * convw_ref[0:1, cs]
        for j in range(1, CONV_W):
            y = y + xx_ref[first + j:first + j + TILE, cs] * convw_ref[j:j + 1, cs]
        return y * jax.nn.sigmoid(y) * valid_col

    def head(h):
        off = h * B_HEAD_DIM
        q = _l2(conv(off)) * (B_HEAD_DIM ** -0.5)
        k = _l2(conv(QB_DIM + off))
        v = conv(2 * QB_DIM + off)
        gb = jnp.broadcast_to(cum_col[:, B_HEADS + h:B_HEADS + h + 1], (TILE, LANES))
        bb = jnp.broadcast_to(beta[:, h:h + 1], (TILE, LANES))
        grow = cum_row[B_HEADS + h:B_HEADS + h + 1, :]
        glast = gb[TILE - 1:TILE, :]
        u, w, qk, qd, kd = yield from _delta_tile(q, k, v, gb, bb, grow, glast, masks)
        s = s_ref[h]
        wq = _mm(jnp.concatenate([w, qd], axis=0), s)
        yield
        v_new = u - wq[:TILE]
        o = wq[TILE:] + _mm(qk, v_new)
        s_ref[h] = s * jnp.exp(glast) + _mm_tn(kd, v_new)
        yield
        zb = zb_ref[:, off:off + B_HEAD_DIM].astype(F32)
        ob_ref[:, off:off + B_HEAD_DIM] = _gated_norm(o, zb, dnorm).astype(ob_ref.dtype)

    _interleave(head(h) for h in range(B_HEADS))

    @pl.when(c == pl.num_programs(1) - 1)
    def _():
        sfin_ref[0] = s_ref[...]


def _delta_prompt(z, ab, abt, hist, s0, conv_w, arow, acol, dnorm, *, nseq, rows_per_seq, front,
                  n_valid):
    tps = rows_per_seq // TILE
    ppt = TILE // PREV_ROWS

    def cur(b, c):
        return b * tps + c

    def prev(b, c):
        return jnp.maximum((b * tps + c) * ppt - 1, 0)

    def zcol(j):
        return pl.BlockSpec((TILE, QB_DIM), lambda b, c: (cur(b, c), j))

    def pcol(j):
        return pl.BlockSpec((PREV_ROWS, QB_DIM), lambda b, c: (prev(b, c), j))

    whole = lambda a: pl.BlockSpec(a.shape, lambda b, c: (0,) * a.ndim)
    state = pl.BlockSpec((1, B_HEADS, B_HEAD_DIM, B_HEAD_DIM), lambda b, c: (b, 0, 0, 0))
    return pl.pallas_call(
        functools.partial(_delta_prompt_kernel, front=front, n_valid=n_valid),
        grid=(nseq, tps),
        in_specs=[pl.BlockSpec((TILE, LANES), lambda b, c: (cur(b, c), 0)),
                  pl.BlockSpec((1, 1, 2 * B_HEADS, TILE), lambda b, c: (b, c, 0, 0)),
                  zcol(0), zcol(1), zcol(2), pcol(0), pcol(1), pcol(2),
                  pl.BlockSpec((1, HIST_ROWS, CONV_DIM), lambda b, c: (b, 0, 0)),
                  zcol(COL_ZB // QB_DIM), state,
                  whole(conv_w), whole(arow), whole(acol), whole(dnorm)],
        out_specs=[pl.BlockSpec((TILE, QB_DIM), lambda b, c: (cur(b, c), 0)), state],
        out_shape=[jax.ShapeDtypeStruct((nseq * rows_per_seq, QB_DIM), BF16),
                   jax.ShapeDtypeStruct(s0.shape, F32)],
        scratch_shapes=[pltpu.VMEM((B_HEADS, B_HEAD_DIM, B_HEAD_DIM), F32),
                        pltpu.VMEM((HIST_ROWS + TILE, CONV_DIM), F32)],
        compiler_params=_params(("parallel", "arbitrary")),
        name="delta_prompt",
    )(ab, abt, z, z, z, z, z, z, hist, z, s0, conv_w, arow, acol, dnorm)


def _delta_sample_kernel(x_ref, ab_ref, abt_ref, hist_ref, s0_ref, convw_ref, arow_ref, alane_ref,
                         dnorm_ref, ob_ref, sout_ref, xx_ref, *, bb, t_new):
    masks = _tile_masks(TOK)
    trow = lax.broadcasted_iota(jnp.int32, (TOK, 1), 0)
    valid_t = (trow < t_new).astype(F32)
    tlane = jnp.bitwise_and(lax.broadcasted_iota(jnp.int32, (TOK, LANES), 1), TOK - 1)
    valid_lane = (tlane < t_new).astype(F32)
    dnorm = dnorm_ref[...]
    first = HIST_ROWS - (CONV_W - 1)

    def stack(a, base):
        return jnp.concatenate(
            [a[:, base + h * B_HEAD_DIM:base + (h + 1) * B_HEAD_DIM] for h in range(B_HEADS)], axis=0)

    def rows_of(col_of):
        return jnp.concatenate(
            [jnp.broadcast_to(col_of(h), (TOK, LANES)) for h in range(B_HEADS)], axis=0)

    def seq(b):
        xx_ref[0:HIST_ROWS, :] = hist_ref[b]
        xx_ref[HIST_ROWS:, :] = x_ref[b, :, 0:CONV_DIM]
        y = xx_ref[first:first + TOK, :] * convw_ref[0:1, :]
        for j in range(1, CONV_W):
            y = y + xx_ref[first + j:first + j + TOK, :] * convw_ref[j:j + 1, :]
        y = y * jax.nn.sigmoid(y) * valid_t
        q = _l2(stack(y, 0)) * (B_HEAD_DIM ** -0.5)
        k = _l2(stack(y, QB_DIM))
        v = stack(y, 2 * QB_DIM)

        ab = ab_ref[b]
        beta = jax.nn.sigmoid(ab)
        g = -jnp.exp(arow_ref[0:1, :]) * _softplus(ab + arow_ref[1:2, :]) * valid_t
        gl = (-jnp.exp(alane_ref[0:1, :]) * _softplus(abt_ref[b] + alane_ref[1:2, :])) * valid_lane
        step = 1
        while step < TOK:
            g = g + jnp.where(trow >= step, pltpu.roll(g, step, 0), 0.0)
            gl = gl + jnp.where(tlane >= step, pltpu.roll(gl, step, 1), 0.0)
            step *= 2
        gb = rows_of(lambda h: g[:, B_HEADS + h:B_HEADS + h + 1])
        bt = rows_of(lambda h: beta[:, h:h + 1])
        glast = rows_of(lambda h: g[TOK - 1:TOK, B_HEADS + h:B_HEADS + h + 1])
        u, w, qk, qd, kd = yield from _delta_tile(q, k, v, gb, bt, gl[0:1, :], glast, masks)

        wq = []
        for h in range(B_HEADS):
            rs = slice(h * TOK, (h + 1) * TOK)
            wq.append(_mm(jnp.concatenate([w[rs], qd[rs]], axis=0), s0_ref[b, h]))
        yield
        v_new = u - jnp.concatenate([x[:TOK] for x in wq], axis=0)
        o = jnp.concatenate([x[TOK:] for x in wq], axis=0) + _mm(qk, v_new)
        yield
        for h in range(B_HEADS):
            rs = slice(h * TOK, (h + 1) * TOK)
            sout_ref[b, h] = (s0_ref[b, h] * jnp.exp(glast[h * TOK:h * TOK + 1, :])
                              + _mm_tn(kd[rs], v_new[rs]))
        ob = _gated_norm(o, stack(x_ref[b], COL_ZB), dnorm)
        for h in range(B_HEADS):
            ob_ref[b, :, h * B_HEAD_DIM:(h + 1) * B_HEAD_DIM] = ob[h * TOK:(h + 1) * TOK]

    _interleave(seq(b) for b in range(bb))


def _delta_sample(xs, ab, abt, hist, s0, conv_w, arow, alane, dnorm, t_new):
    bd = xs.shape[0]
    bb = _pick(bd, 4, 1)
    blk = lambda a: pl.BlockSpec((bb,) + a.shape[1:], lambda i: (i,) + (0,) * (a.ndim - 1))
    whole = lambda a: pl.BlockSpec(a.shape, lambda i: (0,) * a.ndim)
    return pl.pallas_call(
        functools.partial(_delta_sample_kernel, bb=bb, t_new=t_new),
        grid=(bd // bb,),
        in_specs=[blk(xs), blk(ab), blk(abt), blk(hist), blk(s0),
                  whole(conv_w), whole(arow), whole(alane), whole(dnorm)],
        out_specs=[pl.BlockSpec((bb, TOK, QB_DIM), lambda i: (i, 0, 0)), blk(s0)],
        out_shape=[jax.ShapeDtypeStruct((bd, TOK, QB_DIM), F32),
                   jax.ShapeDtypeStruct(s0.shape, F32)],
        scratch_shapes=[pltpu.VMEM((HIST_ROWS + TOK, CONV_DIM), F32)],
        compiler_params=_params(("parallel",)),
        name="delta_sample",
    )(xs, ab, abt, hist, s0, conv_w, arow, alane, dnorm)


def _layer(rows, xp, meta, xs, tables, nb, lp, n_tok, bd, t_new, cache_k, cache_v, state_conv,
           state_delta, norm_mix_pre, norm_mix_post, norm_mlp_pre, norm_mlp_post, w_in, sinks, conv_w,
           a_log, dt_bias, delta_norm, w_branch_a, w_branch_b, w_out, w_up, w_down):
    p_rows = nb * lp

    assert SRC_CONV % MM_TN == 0 and COL_QA % MM_TN == 0 and SRC_BETA % LANES == 0
    n_front, src_front = COL_QA // MM_TN, SRC_CONV // MM_TN
    hn = _norm_in(rows, xp, meta, xs, norm_mix_pre)
    z = _matmul(hn, w_in, BF16, name="in_proj", n=Z_DIM,
                w_col=lambda j: jnp.where(j < n_front, j + src_front, j - n_front))
    ab = _matmul(hn, w_in, F32, name="gate_proj", n=LANES, w_col=lambda j: SRC_BETA // LANES)
    zg = _matmul(hn, _repack_gate_weights(w_in), BF16, name="gate_in_proj")

    qr, kr = _rope(z, tables)
    oa_p = _attn_prompt(qr, kr, z, sinks, nb, lp)

    def sample_pad(a, c0=0, c1=None):
        a = a[p_rows:, c0:c1].astype(F32)
        return jnp.pad(a.reshape(bd, t_new, a.shape[1]), ((0, 0), (0, SAMPLE_ROWS - t_new), (0, 0)))

    n_buf = cache_k.shape[1]
    oa_s, k_s, v_s = _attn_sample(
        sample_pad(qr), sample_pad(kr), sample_pad(z, COL_VA, COL_VA + KVA_DIM),
        cache_k.reshape(bd, n_buf, KVA_DIM), cache_v.reshape(bd, n_buf, KVA_DIM), sinks, t_new)
    oa = jnp.concatenate([oa_p, oa_s[:, :t_new].reshape(bd * t_new, QA_DIM)], axis=0)

    arow = jnp.zeros((2, LANES), F32)
    arow = arow.at[0, B_HEADS:2 * B_HEADS].set(a_log).at[1, B_HEADS:2 * B_HEADS].set(dt_bias)
    acol = jnp.zeros((2 * B_HEADS, LANES), F32)
    acol = acol.at[B_HEADS:, 0].set(a_log).at[B_HEADS:, 1].set(dt_bias)
    alane = jnp.stack([jnp.repeat(a_log, TOK), jnp.repeat(dt_bias, TOK)])
    dnorm = delta_norm.reshape(1, B_HEAD_DIM)
    abt_p = jnp.swapaxes(
        ab[:p_rows].reshape(nb, lp // TILE, TILE, LANES)[..., :2 * B_HEADS], 2, 3)
    ob_p, d_p = _delta_prompt(
        z, ab, abt_p, jnp.zeros((nb, HIST_ROWS, CONV_DIM), F32),
        jnp.zeros((nb, B_HEADS, B_HEAD_DIM, B_HEAD_DIM), F32),
        conv_w, arow, acol, dnorm, nseq=nb, rows_per_seq=lp, front=FRONT, n_valid=n_tok)

    x_tok = sample_pad(z, 0, COL_ZB + QB_DIM)
    ab_s = sample_pad(ab)
    abt_s = jnp.swapaxes(ab_s[..., B_HEADS:2 * B_HEADS], 1, 2).reshape(bd, 1, B_HEADS * TOK)
    hist_s = jnp.pad(state_conv, ((0, 0), (HIST_ROWS - (CONV_W - 1), 0), (0, 0)))
    ob_s, d_s = _delta_sample(x_tok, ab_s, abt_s, hist_s, state_delta, conv_w, arow, alane, dnorm, t_new)
    ob = jnp.concatenate(
        [ob_p, ob_s[:, :t_new].reshape(bd * t_new, QB_DIM).astype(BF16)], axis=0)

    merged = _merge(oa, ob, w_branch_a, w_branch_b, zg)
    y = _matmul(merged, w_out, F32, name="out_proj")
    h1, hn2 = _post_pre(rows, xp, meta, xs, y, norm_mix_post, norm_mlp_pre)
    u = _matmul(hn2, w_up, BF16, relu2=True, name="mlp_up")
    y2 = _matmul(u, w_down, F32, name="mlp_down")
    y_prompt, y_sample = _post_out(rows, h1, y2, norm_mlp_post, xp.shape[1])

    zp = z[:p_rows].reshape(nb, lp, Z_DIM)
    last = FRONT + n_tok
    k_p = kr[:p_rows].reshape(nb, lp, KVA_DIM)[:, last - WINDOW:last]
    v_p = zp[:, last - WINDOW:last, COL_VA:COL_VA + KVA_DIM].astype(F32)
    c_p = zp[:, last - (CONV_W - 1):last, :CONV_DIM].astype(F32)
    x_s = z[p_rows:, :CONV_DIM].astype(F32).reshape(bd, t_new, CONV_DIM)
    c_s = jnp.concatenate([state_conv, x_s], axis=1)[:, -(CONV_W - 1):]
    return y_prompt, y_sample, k_p, v_p, c_p, d_p, k_s, v_s, c_s, d_s


def kernel(x_prompt, x_sample, cache_win_k, cache_win_v, state_conv, state_delta, meta_tokens,
           norm_mix_pre, norm_mix_post, norm_mlp_pre, norm_mlp_post, w_in, sinks, conv_w,
           a_log, dt_bias, delta_norm, w_branch_a, w_branch_b, w_out, w_up, w_down):
    nb, seq, d = x_prompt.shape
    bd, t_new, _ = x_sample.shape
    depth = w_in.shape[0]
    assert depth == 1 and CONV_W - 1 <= t_new <= SAMPLE_ROWS == TOK
    n_tok = N_META + seq
    assert n_tok >= WINDOW
    lp = -(-(FRONT + n_tok) // WINDOW) * WINDOW
    rows = _Rows(nb, seq, lp, bd * t_new)

    pos_p = jnp.maximum(jnp.arange(lp) - FRONT, 0)
    pos = jnp.concatenate([jnp.tile(pos_p, nb), jnp.tile(PAST_LEN + jnp.arange(t_new), bd)])
    tables = _rope_tables(pos)

    l = 0
    y_prompt, y_sample, k_p, v_p, c_p, d_p, k_s, v_s, c_s, d_s = _layer(
        rows, x_prompt, meta_tokens, x_sample.reshape(bd * t_new, d), tables, nb, lp, n_tok, bd,
        t_new, cache_win_k[l], cache_win_v[l], state_conv[l],
        state_delta[l], norm_mix_pre[l], norm_mix_post[l], norm_mlp_pre[l], norm_mlp_post[l],
        w_in[l], sinks[l], conv_w[l], a_log[l], dt_bias[l], delta_norm[l], w_branch_a[l],
        w_branch_b[l], w_out[l], w_up[l], w_down[l])

    y_sample = y_sample.reshape(bd, t_new, d)
    n_buf = cache_win_k.shape[2]
    kv = lambda a, n, r: a.reshape(1, n, r, A_KV_HEADS, A_HEAD_DIM)
    return (y_prompt, y_sample,
            kv(k_p, nb, WINDOW), kv(v_p, nb, WINDOW), c_p[None], d_p[None],
            kv(k_s, bd, n_buf), kv(v_s, bd, n_buf), c_s[None], d_s[None])
```

```python
import functools

import jax
import jax.numpy as jnp
from jax import lax
from jax.experimental import pallas as pl
from jax.experimental.pallas import tpu as pltpu

D_MODEL = 4096
PAST_LEN = 8192
N_META = 16
WINDOW = 128
A_HEADS = 32
A_KV_HEADS = 8
A_GROUP = A_HEADS // A_KV_HEADS
A_HEAD_DIM = 64
ROT_DIM = A_HEAD_DIM // 4
ROPE_THETA = 500000.0
B_HEADS = 16
B_HEAD_DIM = 128
CONV_W = 4
CHUNK = 64
D_FF = 4 * D_MODEL
EPS = 1e-6

QA_DIM = A_HEADS * A_HEAD_DIM
KVA_DIM = A_KV_HEADS * A_HEAD_DIM
QB_DIM = B_HEADS * B_HEAD_DIM
CONV_DIM = 3 * QB_DIM
FRONT = (-N_META) % CHUNK

COL_CONV = 0
COL_ZB = COL_CONV + CONV_DIM
COL_QA = COL_ZB + QB_DIM
COL_KA = COL_QA + QA_DIM
COL_VA = COL_KA + KVA_DIM
Z_DIM = COL_VA + KVA_DIM
SRC_QA = 0
SRC_KA = SRC_QA + QA_DIM
SRC_VA = SRC_KA + KVA_DIM
SRC_CONV = SRC_VA + KVA_DIM
SRC_ZB = SRC_CONV + CONV_DIM
SRC_BETA = SRC_ZB + QB_DIM
SRC_ALPHA = SRC_BETA + B_HEADS
SRC_GA = SRC_ALPHA + B_HEADS
SRC_GB = SRC_GA + D_MODEL
LANES = 128
SUBLANES = 8
RB = FRONT + N_META
HIST_ROWS = 8
PREV_ROWS = 16
NEG = -1e30
VMEM_LIMIT = 56 * 1024 * 1024
HI = lax.Precision.HIGHEST
F32 = jnp.float32
BF16 = jnp.bfloat16


def _pick(n, target, align):
    best = None
    for t in range(align, min(n, target) + 1, align):
        if n % t == 0:
            best = t
    return best if best is not None else n


def _params(sem):
    return pltpu.CompilerParams(dimension_semantics=sem, vmem_limit_bytes=VMEM_LIMIT)


def _rms(x, w):
    return x * lax.rsqrt(jnp.mean(x * x, axis=-1, keepdims=True) + EPS) * w


class _Rows:
    def __init__(self, nb, seq, lp, n_sample):
        assert FRONT + N_META == RB and seq % RB == 0 and lp % RB == 0 and n_sample % RB == 0
        self.nb, self.bps, self.seq_blocks = nb, lp // RB, seq // RB
        self.sample_blocks = n_sample // RB
        self.blocks = nb * self.bps + self.sample_blocks

    def prompt_index(self, r):
        b = jnp.minimum(r // self.bps, self.nb - 1)
        s = jnp.where(r < self.nb * self.bps, jnp.clip(r % self.bps - 1, 0, self.seq_blocks - 1),
                      self.seq_blocks - 1)
        return b, s, 0

    def sample_index(self, r):
        return jnp.clip(r - self.nb * self.bps, 0, self.sample_blocks - 1), 0

    def kinds(self, r):
        rb = r % self.bps
        is_p = r < self.nb * self.bps
        return (is_p & (rb == 0), is_p & (rb >= 1) & (rb <= self.seq_blocks),
                is_p & (rb > self.seq_blocks), jnp.logical_not(is_p))

    def in_specs(self, d):
        return [pl.BlockSpec((1, RB, d), self.prompt_index),
                pl.BlockSpec((N_META, d), lambda r: (0, 0)),
                pl.BlockSpec((RB, d), self.sample_index)]

    def gather(self, r, xp_ref, meta_ref, xs_ref, h_ref):
        is_meta, is_seq, is_zero, is_sample = self.kinds(r)

        @pl.when(is_meta)
        def _():
            h_ref[0:FRONT, :] = jnp.zeros((FRONT, h_ref.shape[1]), F32)
            h_ref[FRONT:, :] = meta_ref[...]

        @pl.when(is_seq)
        def _():
            h_ref[...] = xp_ref[0]

        @pl.when(is_zero)
        def _():
            h_ref[...] = jnp.zeros_like(h_ref)

        @pl.when(is_sample)
        def _():
            h_ref[...] = xs_ref[...]


def _norm_in_kernel(xp_ref, meta_ref, xs_ref, w_ref, o_ref, h_ref, *, rows):
    rows.gather(pl.program_id(0), xp_ref, meta_ref, xs_ref, h_ref)
    o_ref[...] = _rms(h_ref[...], w_ref[...]).astype(o_ref.dtype)


def _norm_in(rows, xp, meta, xs, w):
    d = xp.shape[-1]
    row = pl.BlockSpec((RB, d), lambda r: (r, 0))
    vec = pl.BlockSpec((1, d), lambda r: (0, 0))
    return pl.pallas_call(
        functools.partial(_norm_in_kernel, rows=rows),
        grid=(rows.blocks,),
        in_specs=rows.in_specs(d) + [vec],
        out_specs=row,
        out_shape=jax.ShapeDtypeStruct((rows.blocks * RB, d), BF16),
        scratch_shapes=[pltpu.VMEM((RB, d), F32)],
        compiler_params=_params(("parallel",)),
        name="norm_in",
    )(xp, meta, xs, w.reshape(1, d))


def _post_pre_kernel(xp_ref, meta_ref, xs_ref, y_ref, wpost_ref, wpre_ref, h1_ref, hn_ref, h_ref,
                     *, rows):
    rows.gather(pl.program_id(0), xp_ref, meta_ref, xs_ref, h_ref)
    h1 = h_ref[...] + _rms(y_ref[...], wpost_ref[...])
    h1_ref[...] = h1
    hn_ref[...] = _rms(h1, wpre_ref[...]).astype(hn_ref.dtype)


def _post_pre(rows, xp, meta, xs, y, w_post, w_pre):
    d = xp.shape[-1]
    m = rows.blocks * RB
    row = pl.BlockSpec((RB, d), lambda r: (r, 0))
    vec = pl.BlockSpec((1, d), lambda r: (0, 0))
    return pl.pallas_call(
        functools.partial(_post_pre_kernel, rows=rows),
        grid=(rows.blocks,),
        in_specs=rows.in_specs(d) + [row, vec, vec],
        out_specs=[row, row],
        out_shape=[jax.ShapeDtypeStruct((m, d), F32), jax.ShapeDtypeStruct((m, d), BF16)],
        scratch_shapes=[pltpu.VMEM((RB, d), F32)],
        compiler_params=_params(("parallel",)),
        name="post_pre_norm",
    )(xp, meta, xs, y, w_post.reshape(1, d), w_pre.reshape(1, d))


def _post_out_kernel(h_ref, y_ref, wpost_ref, yp_ref, ys_ref, *, rows):
    _, is_seq, _, is_sample = rows.kinds(pl.program_id(0))
    out = h_ref[...] + _rms(y_ref[...], wpost_ref[...])

    @pl.when(is_seq)
    def _():
        yp_ref[0] = out

    @pl.when(is_sample)
    def _():
        ys_ref[...] = out


def _post_out(rows, h, y, w_post, seq):
    d = h.shape[1]
    row = pl.BlockSpec((RB, d), lambda r: (r, 0))
    vec = pl.BlockSpec((1, d), lambda r: (0, 0))
    return pl.pallas_call(
        functools.partial(_post_out_kernel, rows=rows),
        grid=(rows.blocks,),
        in_specs=[row, row, vec],
        out_specs=[pl.BlockSpec((1, RB, d), rows.prompt_index),
                   pl.BlockSpec((RB, d), rows.sample_index)],
        out_shape=[jax.ShapeDtypeStruct((rows.nb, seq, d), F32),
                   jax.ShapeDtypeStruct((rows.sample_blocks * RB, d), F32)],
        compiler_params=_params(("arbitrary",)),
        name="post_out",
    )(h, y, w_post.reshape(1, d))


def _accumulate(acc_ref, x_ref, w_ref):
    k = pl.program_id(2)

    def prod():
        return jnp.dot(x_ref[...], w_ref[...].astype(BF16), preferred_element_type=F32)

    @pl.when(k == 0)
    def _():
        acc_ref[...] = prod()

    @pl.when(k > 0)
    def _():
        acc_ref[...] += prod()


def _mm_kernel(x_ref, w_ref, o_ref, *scratch, relu2):
    acc_ref = scratch[0] if scratch else o_ref
    _accumulate(acc_ref, x_ref, w_ref)

    if scratch:
        @pl.when(pl.program_id(2) == pl.num_programs(2) - 1)
        def _():
            acc = acc_ref[...]
            if relu2:
                acc = jnp.square(jnp.maximum(acc, 0.0))
            o_ref[...] = acc.astype(o_ref.dtype)


MM_TM = 1792
MM_TN = 1024
MM_TK = 2048


def _matmul(x, w, out_dtype, relu2=False, name="matmul", n=None, w_col=None):
    m, kd = x.shape
    n = w.shape[1] if n is None else n
    tm = _pick(m, MM_TM, 16)
    tn = _pick(n, MM_TN, LANES)
    tk = _pick(kd, MM_TK, LANES)
    w_col = (lambda j: j) if w_col is None else w_col
    assert out_dtype == BF16 or not relu2
    scratch = [pltpu.VMEM((tm, tn), F32)] if out_dtype != F32 else []
    return pl.pallas_call(
        functools.partial(_mm_kernel, relu2=relu2),
        grid=(m // tm, n // tn, kd // tk),
        in_specs=[pl.BlockSpec((tm, tk), lambda i, j, k: (i, k)),
                  pl.BlockSpec((tk, tn), lambda i, j, k: (k, w_col(j)))],
        out_specs=pl.BlockSpec((tm, tn), lambda i, j, k: (i, j)),
        out_shape=jax.ShapeDtypeStruct((m, n), out_dtype),
        scratch_shapes=scratch,
        compiler_params=_params(("parallel", "parallel", "arbitrary")),
        name=name,
    )(x, w)


def _merge_kernel(oa_ref, ob_ref, wa_ref, wb_ref, ga_ref, gb_ref, o_ref, acc_ref):
    def gated(g_ref, x_ref, w_ref):
        prod = jnp.dot(x_ref[...], w_ref[...].astype(BF16), preferred_element_type=F32)
        return jax.nn.sigmoid(g_ref[...].astype(F32)) * prod

    acc_ref[...] = gated(ga_ref, oa_ref, wa_ref)
    o_ref[...] = (acc_ref[...] + gated(gb_ref, ob_ref, wb_ref)).astype(o_ref.dtype)


def _merge(oa, ob, wa, wb, z):
    m, kd = oa.shape
    n = wa.shape[1]
    tm = _pick(m, 1280, 16)
    tn = _pick(n, 512, LANES)
    gb0 = n // tn
    act = pl.BlockSpec((tm, kd), lambda i, j: (i, 0))
    wgt = pl.BlockSpec((kd, tn), lambda i, j: (0, j))
    return pl.pallas_call(
        _merge_kernel,
        grid=(m // tm, n // tn),
        in_specs=[act, act, wgt, wgt,
                  pl.BlockSpec((tm, tn), lambda i, j: (i, j)),
                  pl.BlockSpec((tm, tn), lambda i, j: (i, gb0 + j))],
        out_specs=pl.BlockSpec((tm, tn), lambda i, j: (i, j)),
        out_shape=jax.ShapeDtypeStruct((m, n), BF16),
        scratch_shapes=[pltpu.VMEM((tm, tn), F32)],
        compiler_params=_params(("parallel", "arbitrary")),
        name="branch_merge",
    )(oa, ob, wa, wb, z, z)


def _repack_kernel(a_ref, b_ref, o_ref, *, shift):
    n = a_ref.shape[1]
    body = pltpu.roll(a_ref[...], n - shift, 1)
    tail = pltpu.roll(b_ref[...], LANES - shift, 1)
    lane = lax.broadcasted_iota(jnp.int32, tail.shape, 1)
    o_ref[:, 0:n - LANES] = body[:, 0:n - LANES].astype(o_ref.dtype)
    o_ref[:, n - LANES:] = jnp.where(lane < LANES - shift, body[:, n - LANES:], tail).astype(o_ref.dtype)


def _repack_gate_weights(w_in):
    kd = w_in.shape[0]
    shift = SRC_GA % LANES
    n = 2 * D_MODEL
    tn = MM_TN
    tk = _pick(kd, 512, SUBLANES)
    assert (SRC_GA - shift) % tn == 0 and n % tn == 0 and 0 < shift
    c0 = (SRC_GA - shift) // tn
    return pl.pallas_call(
        functools.partial(_repack_kernel, shift=shift),
        grid=(kd // tk, n // tn),
        in_specs=[pl.BlockSpec((tk, tn), lambda k, j: (k, c0 + j)),
                  pl.BlockSpec((tk, LANES), lambda k, j: (k, (c0 + j + 1) * (tn // LANES)))],
        out_specs=pl.BlockSpec((tk, tn), lambda k, j: (k, j)),
        out_shape=jax.ShapeDtypeStruct((kd, n), BF16),
        compiler_params=_params(("parallel", "parallel")),
        name="repack_gate_weights",
    )(w_in, w_in)


def _rope_tables(pos):
    half = ROT_DIM // 2
    inv_freq = ROPE_THETA ** (-jnp.arange(half, dtype=F32) * (2.0 / ROT_DIM))
    ang = pos.astype(F32)[:, None] * inv_freq[None, :]
    cos, sin = jnp.cos(ang), jnp.sin(ang)
    rows = pos.shape[0]
    rest = A_HEAD_DIM - ROT_DIM
    one = jnp.ones((rows, rest), F32)
    zero = jnp.zeros((rows, rest), F32)
    zh = jnp.zeros((rows, half), F32)
    reps = LANES // A_HEAD_DIM
    c = jnp.tile(jnp.concatenate([cos, cos, one], axis=1), (1, reps))
    s1 = jnp.tile(jnp.concatenate([-sin, zh, zero], axis=1), (1, reps))
    s2 = jnp.tile(jnp.concatenate([zh, sin, zero], axis=1), (1, reps))
    return c, s1, s2


def _rope_kernel(q_ref, k_ref, c_ref, s1_ref, s2_ref, qo_ref, ko_ref):
    half = ROT_DIM // 2
    c, s1, s2 = c_ref[...], s1_ref[...], s2_ref[...]

    def rot(x):
        return (x * c + pltpu.roll(x, LANES - half, 1) * s1 + pltpu.roll(x, half, 1) * s2)

    scale = A_HEAD_DIM ** -0.5
    for j in range(QA_DIM // LANES):
        sl = slice(j * LANES, (j + 1) * LANES)
        qo_ref[:, sl] = (rot(q_ref[:, sl].astype(F32)) * scale).astype(qo_ref.dtype)
    for j in range(KVA_DIM // LANES):
        sl = slice(j * LANES, (j + 1) * LANES)
        ko_ref[:, sl] = rot(k_ref[:, sl].astype(F32))


def _rope(z, tables):
    m = z.shape[0]
    tr = _pick(m, 256, 16)
    tab = pl.BlockSpec((tr, LANES), lambda i: (i, 0))
    return pl.pallas_call(
        _rope_kernel,
        grid=(m // tr,),
        in_specs=[pl.BlockSpec((tr, QA_DIM), lambda i: (i, COL_QA // QA_DIM)),
                  pl.BlockSpec((tr, KVA_DIM), lambda i: (i, COL_KA // KVA_DIM)),
                  tab, tab, tab],
        out_specs=[pl.BlockSpec((tr, QA_DIM), lambda i: (i, 0)),
                   pl.BlockSpec((tr, KVA_DIM), lambda i: (i, 0))],
        out_shape=[jax.ShapeDtypeStruct((m, QA_DIM), BF16),
                   jax.ShapeDtypeStruct((m, KVA_DIM), F32)],
        compiler_params=_params(("parallel",)),
        name="rope",
    )(z, z, *tables)


def _attend_heads(q_of, k_all, v_all, mask, sinks_ref, write, rows):
    mask_g = jnp.concatenate([mask] * A_GROUP, axis=0)

    def group(g):
        cs = slice(g * A_HEAD_DIM, (g + 1) * A_HEAD_DIM)
        kh, vh = k_all[:, cs], v_all[:, cs]
        qg = jnp.concatenate([q_of(g * A_GROUP + j) for j in range(A_GROUP)], axis=0)
        sk = jnp.concatenate(
            [jnp.full((rows, 1), sinks_ref[g * A_GROUP + j], F32) for j in range(A_GROUP)], axis=0)
        s = lax.dot_general(qg, kh, (((1,), (1,)), ((), ())), preferred_element_type=F32)
        yield
        s = jnp.where(mask_g, s, NEG)
        mx = jnp.maximum(jnp.max(s, axis=-1, keepdims=True), sk)
        p = jnp.exp(s - mx)
        den = jnp.sum(p, axis=-1, keepdims=True) + jnp.exp(sk - mx)
        o = jnp.dot(p.astype(BF16), vh, preferred_element_type=F32) / den
        yield
        for j in range(A_GROUP):
            write(g * A_GROUP + j, o[j * rows:(j + 1) * rows])

    return [group(g) for g in range(A_KV_HEADS)]


def _attn_prompt_kernel(sinks_ref, q_ref, kp_ref, kc_ref, vp_ref, vc_ref, o_ref):
    i = pl.program_id(1)
    blk = WINDOW
    k_all = jnp.concatenate([kp_ref[...], kc_ref[...]], axis=0).astype(BF16)
    v_all = jnp.concatenate([vp_ref[...], vc_ref[...]], axis=0).astype(BF16)
    qrow = i * blk + lax.broadcasted_iota(jnp.int32, (blk, 2 * blk), 0)
    krow = (i - 1) * blk + lax.broadcasted_iota(jnp.int32, (blk, 2 * blk), 1)
    diff = qrow - krow
    mask = (diff >= 0) & (diff <= WINDOW) & (krow >= FRONT)

    def q_of(h):
        return q_ref[:, h * A_HEAD_DIM:(h + 1) * A_HEAD_DIM]

    def write(h, o):
        o_ref[:, h * A_HEAD_DIM:(h + 1) * A_HEAD_DIM] = o.astype(o_ref.dtype)

    _interleave(_attend_heads(q_of, k_all, v_all, mask, sinks_ref, write, blk))


def _attn_prompt(qr, kr, z, sinks, nb, lp):
    blk = WINDOW
    nblk = lp // blk
    vcol = COL_VA // KVA_DIM

    def cur(b, i):
        return b * nblk + i

    def prev(b, i):
        return b * nblk + jnp.maximum(i - 1, 0)

    return pl.pallas_call(
        _attn_prompt_kernel,
        grid=(nb, nblk),
        in_specs=[pl.BlockSpec(memory_space=pltpu.SMEM),
                  pl.BlockSpec((blk, QA_DIM), lambda b, i: (cur(b, i), 0)),
                  pl.BlockSpec((blk, KVA_DIM), lambda b, i: (prev(b, i), 0)),
                  pl.BlockSpec((blk, KVA_DIM), lambda b, i: (cur(b, i), 0)),
                  pl.BlockSpec((blk, KVA_DIM), lambda b, i: (prev(b, i), vcol)),
                  pl.BlockSpec((blk, KVA_DIM), lambda b, i: (cur(b, i), vcol))],
        out_specs=pl.BlockSpec((blk, QA_DIM), lambda b, i: (cur(b, i), 0)),
        out_shape=jax.ShapeDtypeStruct((qr.shape[0], QA_DIM), BF16),
        compiler_params=_params(("parallel", "parallel")),
        name="attn_prompt",
    )(sinks, qr, kr, kr, z, z)


SAMPLE_ROWS = 8
SAMPLE_KEYS = 256


def _attn_sample_kernel(sinks_ref, q_ref, kn_ref, vn_ref, ck_ref, cv_ref, _, o_ref, ko_ref, vo_ref,
                        kall_ref, vall_ref, o_scr, *, bb, n_buf, t_new):
    pad = SAMPLE_KEYS - n_buf - SAMPLE_ROWS
    qrow = lax.broadcasted_iota(jnp.int32, (SAMPLE_ROWS, SAMPLE_KEYS), 0)
    kcol = lax.broadcasted_iota(jnp.int32, (SAMPLE_ROWS, SAMPLE_KEYS), 1)
    diff = qrow + n_buf - kcol
    mask = (diff >= 0) & (diff <= WINDOW) & (kcol < n_buf + t_new)
    gens = []
    for b in range(bb):
        kall_ref[b, 0:n_buf, :] = ck_ref[b]
        kall_ref[b, n_buf:n_buf + SAMPLE_ROWS, :] = kn_ref[b]
        kall_ref[b, n_buf + SAMPLE_ROWS:, :] = jnp.zeros((pad, KVA_DIM), F32)
        vall_ref[b, 0:n_buf, :] = cv_ref[b]
        vall_ref[b, n_buf:n_buf + SAMPLE_ROWS, :] = vn_ref[b]
        vall_ref[b, n_buf + SAMPLE_ROWS:, :] = jnp.zeros((pad, KVA_DIM), F32)
        ko_ref[b] = kall_ref[b, t_new:t_new + n_buf, :]
        vo_ref[b] = vall_ref[b, t_new:t_new + n_buf, :]

        def q_of(h, b=b):
            return q_ref[b, :, h * A_HEAD_DIM:(h + 1) * A_HEAD_DIM].astype(BF16)

        def write(h, o, b=b):
            o_scr[b * t_new:(b + 1) * t_new, h * A_HEAD_DIM:(h + 1) * A_HEAD_DIM] = o[0:t_new]

        gens += _attend_heads(q_of, kall_ref[b].astype(BF16), vall_ref[b].astype(BF16), mask,
                              sinks_ref, write, SAMPLE_ROWS)
    _interleave(gens)
    o_ref[...] = o_scr[...].astype(o_ref.dtype)


def _attn_sample(qn, kn, vn, cache_k, cache_v, sinks, t_new, o_full, row0):
    bd, n_buf, _ = cache_k.shape
    bb = _pick(bd, 8, 1)
    rows = bb * t_new
    assert row0 % rows == 0 and rows % 16 == 0
    blk3 = lambda r, c: pl.BlockSpec((bb, r, c), lambda i: (i, 0, 0))
    return pl.pallas_call(
        functools.partial(_attn_sample_kernel, bb=bb, n_buf=n_buf, t_new=t_new),
        grid=(bd // bb,),
        in_specs=[pl.BlockSpec(memory_space=pltpu.SMEM),
                  blk3(SAMPLE_ROWS, QA_DIM), blk3(SAMPLE_ROWS, KVA_DIM), blk3(SAMPLE_ROWS, KVA_DIM),
                  blk3(n_buf, KVA_DIM), blk3(n_buf, KVA_DIM),
                  pl.BlockSpec(memory_space=pl.ANY)],
        out_specs=[pl.BlockSpec((rows, QA_DIM), lambda i: (row0 // rows + i, 0)),
                   blk3(n_buf, KVA_DIM), blk3(n_buf, KVA_DIM)],
        out_shape=[jax.ShapeDtypeStruct(o_full.shape, o_full.dtype),
                   jax.ShapeDtypeStruct((bd, n_buf, KVA_DIM), F32),
                   jax.ShapeDtypeStruct((bd, n_buf, KVA_DIM), F32)],
        scratch_shapes=[pltpu.VMEM((bb, SAMPLE_KEYS, KVA_DIM), F32),
                        pltpu.VMEM((bb, SAMPLE_KEYS, KVA_DIM), F32),
                        pltpu.VMEM((rows, QA_DIM), F32)],
        input_output_aliases={6: 0},
        compiler_params=_params(("parallel",)),
        name="attn_sample",
    )(sinks, qn, kn, vn, cache_k, cache_v, o_full)


TILE = 128
TOK = SUBLANES


def _softplus(x):
    return jnp.maximum(x, 0.0) + jnp.log(1.0 + jnp.exp(-jnp.abs(x)))


def _bf(x):
    return x.astype(BF16)


def _mm(a, b):
    return jnp.dot(_bf(a), _bf(b), preferred_element_type=F32)


def _mm_nt(a, b):
    return lax.dot_general(_bf(a), _bf(b), (((1,), (1,)), ((), ())), preferred_element_type=F32)


def _mm_tn(a, b):
    return lax.dot_general(a, b, (((0,), (0,)), ((), ())), preferred_element_type=F32)


def _tile_masks(block):
    r = lax.broadcasted_iota(jnp.int32, (TILE, TILE), 0)
    q = lax.broadcasted_iota(jnp.int32, (TILE, TILE), 1)

    def same(size):
        sh = size.bit_length() - 1
        return jnp.right_shift(r, sh) == jnp.right_shift(q, sh)

    tri = same(block) & (r >= q)
    strict = same(block) & (r > q)
    levels = []
    size = SUBLANES
    while size < block:
        levels.append(same(2 * size) & jnp.logical_not(same(size)))
        size *= 2
    return tri, strict, same(SUBLANES), levels


def _inverse_minus_eye(m, same8, levels):
    n1 = -jnp.where(same8, m, 0.0)
    n1b = _bf(n1)
    n2 = _mm(n1b, n1b)
    yield
    n2b = _bf(n2)
    n4 = _mm(n2b, n2b)
    n12 = _mm(n1b, n2b)
    yield
    a = n1 + n2 + n12
    d = a + n4 + _mm(a, n4)
    yield
    for mask in levels:
        off = jnp.where(mask, m, 0.0)
        db = _bf(d)
        p = off + _mm(off, db)
        yield
        d = d - (p + _mm(db, p))
        yield
    return d


def _delta_tile(q, k, v, gb, bb, grow, glast, masks):
    tri, strict, same8, levels = masks
    decay = jnp.where(tri, jnp.exp(jnp.where(tri, gb - grow, 0.0)), 0.0)
    eg = jnp.exp(gb)
    kb = k * bb
    aq = _mm_nt(jnp.concatenate([kb, q], axis=0), k)
    yield
    m = jnp.where(strict, aq[:TILE] * decay, 0.0)
    qk = aq[TILE:] * decay
    d = yield from _inverse_minus_eye(m, same8, levels)
    rhs = jnp.concatenate([v * bb, kb * eg], axis=1)
    sol = rhs + _mm(d, rhs)
    yield
    return sol[:, :B_HEAD_DIM], sol[:, B_HEAD_DIM:], qk, q * eg, k * jnp.exp(glast - gb)


def _interleave(gens):
    live = list(gens)
    while live:
        nxt = []
        for g in live:
            try:
                next(g)
                nxt.append(g)
            except StopIteration:
                pass
        live = nxt


def _l2(x):
    return x * lax.rsqrt(jnp.sum(x * x, axis=-1, keepdims=True) + EPS)


def _gated_norm(o, zb, dnorm):
    return _rms(o, dnorm) * (zb * jax.nn.sigmoid(zb))


def _delta_prompt_kernel(ab_ref, abt_ref, xq_ref, xk_ref, xv_ref, pq_ref, pk_ref, pv_ref, hist_ref,
                         zb_ref, s0_ref, convw_ref, arow_ref, acol_ref, dnorm_ref,
                         ob_ref, sfin_ref, s_ref, xx_ref, *, front, n_valid):
    c = pl.program_id(1)

    @pl.when(c == 0)
    def _():
        s_ref[...] = s0_ref[0]
        xx_ref[0:HIST_ROWS, :] = hist_ref[0]

    @pl.when(c > 0)
    def _():
        lo = PREV_ROWS - HIST_ROWS
        xx_ref[0:HIST_ROWS, 0:QB_DIM] = pq_ref[lo:, :].astype(F32)
        xx_ref[0:HIST_ROWS, QB_DIM:2 * QB_DIM] = pk_ref[lo:, :].astype(F32)
        xx_ref[0:HIST_ROWS, 2 * QB_DIM:] = pv_ref[lo:, :].astype(F32)

    xx_ref[HIST_ROWS:, 0:QB_DIM] = xq_ref[...].astype(F32)
    xx_ref[HIST_ROWS:, QB_DIM:2 * QB_DIM] = xk_ref[...].astype(F32)
    xx_ref[HIST_ROWS:, 2 * QB_DIM:] = xv_ref[...].astype(F32)

    pos_col = c * TILE + lax.broadcasted_iota(jnp.int32, (TILE, 1), 0)
    valid_col = ((pos_col >= front) & (pos_col < front + n_valid)).astype(F32)
    pos_row = c * TILE + lax.broadcasted_iota(jnp.int32, (1, TILE), 1)
    valid_row = ((pos_row >= front) & (pos_row < front + n_valid)).astype(F32)

    ab = ab_ref[...]
    beta = jax.nn.sigmoid(ab)
    g_col = -jnp.exp(arow_ref[0:1, :]) * _softplus(ab + arow_ref[1:2, :]) * valid_col
    g_row = -jnp.exp(acol_ref[:, 0:1]) * _softplus(abt_ref[0, 0] + acol_ref[:, 1:2]) * valid_row
    ri = lax.broadcasted_iota(jnp.int32, (TILE, TILE), 0)
    ci = lax.broadcasted_iota(jnp.int32, (TILE, TILE), 1)
    cum_col = jnp.dot((ri >= ci).astype(F32), g_col, precision=HI, preferred_element_type=F32)
    cum_row = jnp.dot(g_row, (ri <= ci).astype(F32), precision=HI, preferred_element_type=F32)
    masks = _tile_masks(TILE)
    dnorm = dnorm_ref[...]

    def conv(col):
        cs = slice(col, col + B_HEAD_DIM)
        x = xx_ref[:, cs]
        y = x[HIST_ROWS:] * convw_ref[CONV_W - 1:CONV_W, cs]
        for j in range(CONV_W - 1):
            shifted = pltpu.roll(x, CONV_W - 1 - j, 0)[HIST_ROWS:]
            y = y + shifted * convw_ref[j:j + 1, cs]
        return y * jax.nn.sigmoid(y) * valid_col

    def head(h):
        off = h * B_HEAD_DIM
        q = _l2(conv(off)) * (B_HEAD_DIM ** -0.5)
        k = _l2(conv(QB_DIM + off))
        v = conv(2 * QB_DIM + off)
        gb = jnp.broadcast_to(cum_col[:, B_HEADS + h:B_HEADS + h + 1], (TILE, LANES))
        bb = jnp.broadcast_to(beta[:, h:h + 1], (TILE, LANES))
        grow = cum_row[B_HEADS + h:B_HEADS + h + 1, :]
        glast = gb[TILE - 1:TILE, :]
        u, w, qk, qd, kd = yield from _delta_tile(q, k, v, gb, bb, grow, glast, masks)
        s = s_ref[h]
        wq = _mm(jnp.concatenate([w, qd], axis=0), s)
        yield
        v_new = u - wq[:TILE]
        o = wq[TILE:] + _mm(qk, v_new)
        s_ref[h] = s * jnp.exp(glast) + _mm_tn(kd, v_new)
        yield
        zb = zb_ref[:, off:off + B_HEAD_DIM].astype(F32)
        ob_ref[:, off:off + B_HEAD_DIM] = _gated_norm(o, zb, dnorm).astype(ob_ref.dtype)

    _interleave(head(h) for h in range(B_HEADS))

    @pl.when(c == pl.num_programs(1) - 1)
    def _():
        sfin_ref[0] = s_ref[...]


def _delta_prompt(z, ab, abt, hist, s0, conv_w, arow, acol, dnorm, *, nseq, rows_per_seq, front,
                  n_valid):
    tps = rows_per_seq // TILE
    ppt = TILE // PREV_ROWS

    def cur(b, c):
        return b * tps + c

    def prev(b, c):
        return jnp.maximum((b * tps + c) * ppt - 1, 0)

    def zcol(j):
        return pl.BlockSpec((TILE, QB_DIM), lambda b, c: (cur(b, c), j))

    def pcol(j):
        return pl.BlockSpec((PREV_ROWS, QB_DIM), lambda b, c: (prev(b, c), j))

    whole = lambda a: pl.BlockSpec(a.shape, lambda b, c: (0,) * a.ndim)
    state = pl.BlockSpec((1, B_HEADS, B_HEAD_DIM, B_HEAD_DIM), lambda b, c: (b, 0, 0, 0))
    return pl.pallas_call(
        functools.partial(_delta_prompt_kernel, front=front, n_valid=n_valid),
        grid=(nseq, tps),
        in_specs=[pl.BlockSpec((TILE, LANES), lambda b, c: (cur(b, c), 0)),
                  pl.BlockSpec((1, 1, 2 * B_HEADS, TILE), lambda b, c: (b, c, 0, 0)),
                  zcol(0), zcol(1), zcol(2), pcol(0), pcol(1), pcol(2),
                  pl.BlockSpec((1, HIST_ROWS, CONV_DIM), lambda b, c: (b, 0, 0)),
                  zcol(COL_ZB // QB_DIM), state,
                  whole(conv_w), whole(arow), whole(acol), whole(dnorm)],
        out_specs=[pl.BlockSpec((TILE, QB_DIM), lambda b, c: (cur(b, c), 0)), state],
        out_shape=[jax.ShapeDtypeStruct((z.shape[0], QB_DIM), BF16),
                   jax.ShapeDtypeStruct(s0.shape, F32)],
        scratch_shapes=[pltpu.VMEM((B_HEADS, B_HEAD_DIM, B_HEAD_DIM), F32),
                        pltpu.VMEM((HIST_ROWS + TILE, CONV_DIM), F32)],
        compiler_params=_params(("parallel", "arbitrary")),
        name="delta_prompt",
    )(ab, abt, z, z, z, z, z, z, hist, z, s0, conv_w, arow, acol, dnorm)


def _delta_sample_kernel(x_ref, ab_ref, abt_ref, hist_ref, s0_ref, convw_ref, arow_ref, alane_ref,
                         dnorm_ref, _, ob_ref, sout_ref, xx_ref, ob_scr, *, bb, t_new):
    masks = _tile_masks(TOK)
    trow = lax.broadcasted_iota(jnp.int32, (TOK, 1), 0)
    valid_t = (trow < t_new).astype(F32)
    tlane = jnp.bitwise_and(lax.broadcasted_iota(jnp.int32, (TOK, LANES), 1), TOK - 1)
    valid_lane = (tlane < t_new).astype(F32)
    dnorm = dnorm_ref[...]
    first = HIST_ROWS - (CONV_W - 1)

    def stack(a, base):
        return jnp.concatenate(
            [a[:, base + h * B_HEAD_DIM:base + (h + 1) * B_HEAD_DIM] for h in range(B_HEADS)], axis=0)

    def rows_of(col_of):
        return jnp.concatenate(
            [jnp.broadcast_to(col_of(h), (TOK, LANES)) for h in range(B_HEADS)], axis=0)

    def seq(b):
        xx_ref[0:HIST_ROWS, :] = hist_ref[b]
        xx_ref[HIST_ROWS:, :] = x_ref[b, :, 0:CONV_DIM]
        y = xx_ref[first:first + TOK, :] * convw_ref[0:1, :]
        for j in range(1, CONV_W):
            y = y + xx_ref[first + j:first + j + TOK, :] * convw_ref[j:j + 1, :]
        y = y * jax.nn.sigmoid(y) * valid_t
        q = _l2(stack(y, 0)) * (B_HEAD_DIM ** -0.5)
        k = _l2(stack(y, QB_DIM))
        v = stack(y, 2 * QB_DIM)

        ab = ab_ref[b]
        beta = jax.nn.sigmoid(ab)
        g = -jnp.exp(arow_ref[0:1, :]) * _softplus(ab + arow_ref[1:2, :]) * valid_t
        gl = (-jnp.exp(alane_ref[0:1, :]) * _softplus(abt_ref[b] + alane_ref[1:2, :])) * valid_lane
        step = 1
        while step < TOK:
            g = g + jnp.where(trow >= step, pltpu.roll(g, step, 0), 0.0)
            gl = gl + jnp.where(tlane >= step, pltpu.roll(gl, step, 1), 0.0)
            step *= 2
        gb = rows_of(lambda h: g[:, B_HEADS + h:B_HEADS + h + 1])
        bt = rows_of(lambda h: beta[:, h:h + 1])
        glast = rows_of(lambda h: g[TOK - 1:TOK, B_HEADS + h:B_HEADS + h + 1])
        u, w, qk, qd, kd = yield from _delta_tile(q, k, v, gb, bt, gl[0:1, :], glast, masks)

        wq = []
        for h in range(B_HEADS):
            rs = slice(h * TOK, (h + 1) * TOK)
            wq.append(_mm(jnp.concatenate([w[rs], qd[rs]], axis=0), s0_ref[b, h]))
        yield
        v_new = u - jnp.concatenate([x[:TOK] for x in wq], axis=0)
        o = jnp.concatenate([x[TOK:] for x in wq], axis=0) + _mm(qk, v_new)
        yield
        for h in range(B_HEADS):
            rs = slice(h * TOK, (h + 1) * TOK)
            sout_ref[b, h] = (s0_ref[b, h] * jnp.exp(glast[h * TOK:h * TOK + 1, :])
                              + _mm_tn(kd[rs], v_new[rs]))
        ob = _gated_norm(o, stack(x_ref[b], COL_ZB), dnorm)
        for h in range(B_HEADS):
            ob_scr[b * t_new:(b + 1) * t_new, h * B_HEAD_DIM:(h + 1) * B_HEAD_DIM] = (
                ob[h * TOK:h * TOK + t_new])

    _interleave(seq(b) for b in range(bb))
    ob_ref[...] = ob_scr[...].astype(ob_ref.dtype)


def _delta_sample(xs, ab, abt, hist, s0, conv_w, arow, alane, dnorm, t_new, o_full, row0):
    bd = xs.shape[0]
    bb = _pick(bd, 4, 1)
    rows = bb * t_new
    assert row0 % rows == 0 and rows % 16 == 0
    blk = lambda a: pl.BlockSpec((bb,) + a.shape[1:], lambda i: (i,) + (0,) * (a.ndim - 1))
    whole = lambda a: pl.BlockSpec(a.shape, lambda i: (0,) * a.ndim)
    return pl.pallas_call(
        functools.partial(_delta_sample_kernel, bb=bb, t_new=t_new),
        grid=(bd // bb,),
        in_specs=[blk(xs), blk(ab), blk(abt), blk(hist), blk(s0),
                  whole(conv_w), whole(arow), whole(alane), whole(dnorm),
                  pl.BlockSpec(memory_space=pl.ANY)],
        out_specs=[pl.BlockSpec((rows, QB_DIM), lambda i: (row0 // rows + i, 0)), blk(s0)],
        out_shape=[jax.ShapeDtypeStruct(o_full.shape, o_full.dtype),
                   jax.ShapeDtypeStruct(s0.shape, F32)],
        scratch_shapes=[pltpu.VMEM((HIST_ROWS + TOK, CONV_DIM), F32),
                        pltpu.VMEM((rows, QB_DIM), F32)],
        input_output_aliases={9: 0},
        compiler_params=_params(("parallel",)),
        name="delta_sample",
    )(xs, ab, abt, hist, s0, conv_w, arow, alane, dnorm, o_full)


def _layer(rows, xp, meta, xs, tables, nb, lp, n_tok, bd, t_new, cache_k, cache_v, state_conv,
           state_delta, norm_mix_pre, norm_mix_post, norm_mlp_pre, norm_mlp_post, w_in, sinks, conv_w,
           a_log, dt_bias, delta_norm, w_branch_a, w_branch_b, w_out, w_up, w_down):
    p_rows = nb * lp

    assert SRC_CONV % MM_TN == 0 and COL_QA % MM_TN == 0 and SRC_BETA % LANES == 0
    n_front, src_front = COL_QA // MM_TN, SRC_CONV // MM_TN
    hn = _norm_in(rows, xp, meta, xs, norm_mix_pre)
    z = _matmul(hn, w_in, BF16, name="in_proj", n=Z_DIM,
                w_col=lambda j: jnp.where(j < n_front, j + src_front, j - n_front))
    ab = _matmul(hn, w_in, F32, name="gate_proj", n=LANES, w_col=lambda j: SRC_BETA // LANES)
    zg = _matmul(hn, _repack_gate_weights(w_in), BF16, name="gate_in_proj")

    qr, kr = _rope(z, tables)
    oa_p = _attn_prompt(qr, kr, z, sinks, nb, lp)

    def sample_pad(a, c0=0, c1=None):
        a = a[p_rows:, c0:c1].astype(F32)
        return jnp.pad(a.reshape(bd, t_new, a.shape[1]), ((0, 0), (0, SAMPLE_ROWS - t_new), (0, 0)))

    n_buf = cache_k.shape[1]
    oa, k_s, v_s = _attn_sample(
        sample_pad(qr), sample_pad(kr), sample_pad(z, COL_VA, COL_VA + KVA_DIM),
        cache_k.reshape(bd, n_buf, KVA_DIM), cache_v.reshape(bd, n_buf, KVA_DIM), sinks, t_new,
        oa_p, p_rows)

    arow = jnp.zeros((2, LANES), F32)
    arow = arow.at[0, B_HEADS:2 * B_HEADS].set(a_log).at[1, B_HEADS:2 * B_HEADS].set(dt_bias)
    acol = jnp.zeros((2 * B_HEADS, LANES), F32)
    acol = acol.at[B_HEADS:, 0].set(a_log).at[B_HEADS:, 1].set(dt_bias)
    alane = jnp.stack([jnp.repeat(a_log, TOK), jnp.repeat(dt_bias, TOK)])
    dnorm = delta_norm.reshape(1, B_HEAD_DIM)
    abt_p = jnp.swapaxes(
        ab[:p_rows].reshape(nb, lp // TILE, TILE, LANES)[..., :2 * B_HEADS], 2, 3)
    ob_p, d_p = _delta_prompt(
        z, ab, abt_p, jnp.zeros((nb, HIST_ROWS, CONV_DIM), F32),
        jnp.zeros((nb, B_HEADS, B_HEAD_DIM, B_HEAD_DIM), F32),
        conv_w, arow, acol, dnorm, nseq=nb, rows_per_seq=lp, front=FRONT, n_valid=n_tok)

    x_tok = sample_pad(z, 0, COL_ZB + QB_DIM)
    ab_s = sample_pad(ab)
    abt_s = jnp.swapaxes(ab_s[..., B_HEADS:2 * B_HEADS], 1, 2).reshape(bd, 1, B_HEADS * TOK)
    hist_s = jnp.pad(state_conv, ((0, 0), (HIST_ROWS - (CONV_W - 1), 0), (0, 0)))
    ob, d_s = _delta_sample(x_tok, ab_s, abt_s, hist_s, state_delta, conv_w, arow, alane, dnorm, t_new,
                            ob_p, p_rows)

    merged = _merge(oa, ob, w_branch_a, w_branch_b, zg)
    y = _matmul(merged, w_out, F32, name="out_proj")
    h1, hn2 = _post_pre(rows, xp, meta, xs, y, norm_mix_post, norm_mlp_pre)
    u = _matmul(hn2, w_up, BF16, relu2=True, name="mlp_up")
    y2 = _matmul(u, w_down, F32, name="mlp_down")
    y_prompt, y_sample = _post_out(rows, h1, y2, norm_mlp_post, xp.shape[1])

    last = FRONT + n_tok

    def tail_rows(a, n, c0, c1):
        return jnp.stack([a[b * lp + last - n:b * lp + last, c0:c1] for b in range(nb)]).astype(F32)

    k_p = tail_rows(kr, WINDOW, 0, KVA_DIM)
    v_p = tail_rows(z, WINDOW, COL_VA, COL_VA + KVA_DIM)
    c_p = tail_rows(z, CONV_W - 1, 0, CONV_DIM)
    x_s = z[p_rows:, :CONV_DIM].astype(F32).reshape(bd, t_new, CONV_DIM)
    c_s = jnp.concatenate([state_conv, x_s], axis=1)[:, -(CONV_W - 1):]
    return y_prompt, y_sample, k_p, v_p, c_p, d_p, k_s, v_s, c_s, d_s


def kernel(x_prompt, x_sample, cache_win_k, cache_win_v, state_conv, state_delta, meta_tokens,
           norm_mix_pre, norm_mix_post, norm_mlp_pre, norm_mlp_post, w_in, sinks, conv_w,
           a_log, dt_bias, delta_norm, w_branch_a, w_branch_b, w_out, w_up, w_down):
    nb, seq, d = x_prompt.shape
    bd, t_new, _ = x_sample.shape
    depth = w_in.shape[0]
    assert depth == 1 and CONV_W - 1 <= t_new <= SAMPLE_ROWS == TOK
    n_tok = N_META + seq
    assert n_tok >= WINDOW
    lp = -(-(FRONT + n_tok) // WINDOW) * WINDOW
    rows = _Rows(nb, seq, lp, bd * t_new)

    pos_p = jnp.maximum(jnp.arange(lp) - FRONT, 0)
    pos = jnp.concatenate([jnp.tile(pos_p, nb), jnp.tile(PAST_LEN + jnp.arange(t_new), bd)])
    tables = _rope_tables(pos)

    l = 0
    y_prompt, y_sample, k_p, v_p, c_p, d_p, k_s, v_s, c_s, d_s = _layer(
        rows, x_prompt, meta_tokens, x_sample.reshape(bd * t_new, d), tables, nb, lp, n_tok, bd,
        t_new, cache_win_k[l], cache_win_v[l], state_conv[l],
        state_delta[l], norm_mix_pre[l], norm_mix_post[l], norm_mlp_pre[l], norm_mlp_post[l],
        w_in[l], sinks[l], conv_w[l], a_log[l], dt_bias[l], delta_norm[l], w_branch_a[l],
        w_branch_b[l], w_out[l], w_up[l], w_down[l])

    y_sample = y_sample.reshape(bd, t_new, d)
    n_buf = cache_win_k.shape[2]
    kv = lambda a, n, r: a.reshape(1, n, r, A_KV_HEADS, A_HEAD_DIM)
    return (y_prompt, y_sample,
            kv(k_p, nb, WINDOW), kv(v_p, nb, WINDOW), c_p[None], d_p[None],
            kv(k_s, bd, n_buf), kv(v_s, bd, n_buf), c_s[None], d_s[None])
```

```python
import functools

import jax
import jax.numpy as jnp
from jax import lax
from jax.experimental import pallas as pl
from jax.experimental.pallas import tpu as pltpu

D_MODEL = 4096
PAST_LEN = 8192
N_META = 16
WINDOW = 128
A_HEADS = 32
A_KV_HEADS = 8
A_GROUP = A_HEADS // A_KV_HEADS
A_HEAD_DIM = 64
ROT_DIM = A_HEAD_DIM // 4
ROPE_THETA = 500000.0
B_HEADS = 16
B_HEAD_DIM = 128
CONV_W = 4
CHUNK = 64
D_FF = 4 * D_MODEL
EPS = 1e-6

QA_DIM = A_HEADS * A_HEAD_DIM
KVA_DIM = A_KV_HEADS * A_HEAD_DIM
QB_DIM = B_HEADS * B_HEAD_DIM
CONV_DIM = 3 * QB_DIM
FRONT = (-N_META) % CHUNK

COL_CONV = 0
COL_ZB = COL_CONV + CONV_DIM
COL_QA = COL_ZB + QB_DIM
COL_KA = COL_QA + QA_DIM
COL_VA = COL_KA + KVA_DIM
Z_DIM = COL_VA + KVA_DIM
SRC_QA = 0
SRC_KA = SRC_QA + QA_DIM
SRC_VA = SRC_KA + KVA_DIM
SRC_CONV = SRC_VA + KVA_DIM
SRC_ZB = SRC_CONV + CONV_DIM
SRC_BETA = SRC_ZB + QB_DIM
SRC_ALPHA = SRC_BETA + B_HEADS
SRC_GA = SRC_ALPHA + B_HEADS
SRC_GB = SRC_GA + D_MODEL
LANES = 128
SUBLANES = 8
RB = FRONT + N_META
HIST_ROWS = 8
PREV_ROWS = 16
NEG = -1e30
VMEM_LIMIT = 56 * 1024 * 1024
HI = lax.Precision.HIGHEST
F32 = jnp.float32
BF16 = jnp.bfloat16


def _pick(n, target, align):
    best = None
    for t in range(align, min(n, target) + 1, align):
        if n % t == 0:
            best = t
    return best if best is not None else n


def _params(sem):
    return pltpu.CompilerParams(dimension_semantics=sem, vmem_limit_bytes=VMEM_LIMIT)


def _rms(x, w):
    return x * lax.rsqrt(jnp.mean(x * x, axis=-1, keepdims=True) + EPS) * w


class _Rows:
    def __init__(self, nb, seq, lp, n_sample):
        assert FRONT + N_META == RB and seq % RB == 0 and lp % RB == 0 and n_sample % RB == 0
        self.nb, self.bps, self.seq_blocks = nb, lp // RB, seq // RB
        self.sample_blocks = n_sample // RB
        self.blocks = nb * self.bps + self.sample_blocks

    def prompt_index(self, r):
        b = jnp.minimum(r // self.bps, self.nb - 1)
        s = jnp.where(r < self.nb * self.bps, jnp.clip(r % self.bps - 1, 0, self.seq_blocks - 1),
                      self.seq_blocks - 1)
        return b, s, 0

    def sample_index(self, r):
        return jnp.clip(r - self.nb * self.bps, 0, self.sample_blocks - 1), 0

    def kinds(self, r):
        rb = r % self.bps
        is_p = r < self.nb * self.bps
        return (is_p & (rb == 0), is_p & (rb >= 1) & (rb <= self.seq_blocks),
                is_p & (rb > self.seq_blocks), jnp.logical_not(is_p))

    def in_specs(self, d):
        return [pl.BlockSpec((1, RB, d), self.prompt_index),
                pl.BlockSpec((N_META, d), lambda r: (0, 0)),
                pl.BlockSpec((RB, d), self.sample_index)]

    def gather(self, r, xp_ref, meta_ref, xs_ref, h_ref):
        is_meta, is_seq, is_zero, is_sample = self.kinds(r)

        @pl.when(is_meta)
        def _():
            h_ref[0:FRONT, :] = jnp.zeros((FRONT, h_ref.shape[1]), F32)
            h_ref[FRONT:, :] = meta_ref[...]

        @pl.when(is_seq)
        def _():
            h_ref[...] = xp_ref[0]

        @pl.when(is_zero)
        def _():
            h_ref[...] = jnp.zeros_like(h_ref)

        @pl.when(is_sample)
        def _():
            h_ref[...] = xs_ref[...]


def _norm_in_kernel(xp_ref, meta_ref, xs_ref, w_ref, o_ref, h_ref, *, rows):
    rows.gather(pl.program_id(0), xp_ref, meta_ref, xs_ref, h_ref)
    o_ref[...] = _rms(h_ref[...], w_ref[...]).astype(o_ref.dtype)


def _norm_in(rows, xp, meta, xs, w):
    d = xp.shape[-1]
    row = pl.BlockSpec((RB, d), lambda r: (r, 0))
    vec = pl.BlockSpec((1, d), lambda r: (0, 0))
    return pl.pallas_call(
        functools.partial(_norm_in_kernel, rows=rows),
        grid=(rows.blocks,),
        in_specs=rows.in_specs(d) + [vec],
        out_specs=row,
        out_shape=jax.ShapeDtypeStruct((rows.blocks * RB, d), BF16),
        scratch_shapes=[pltpu.VMEM((RB, d), F32)],
        compiler_params=_params(("parallel",)),
        name="norm_in",
    )(xp, meta, xs, w.reshape(1, d))


def _post_pre_kernel(xp_ref, meta_ref, xs_ref, y_ref, wpost_ref, wpre_ref, h1_ref, hn_ref, h_ref,
                     *, rows):
    rows.gather(pl.program_id(0), xp_ref, meta_ref, xs_ref, h_ref)
    h1 = h_ref[...] + _rms(y_ref[...], wpost_ref[...])
    h1_ref[...] = h1
    hn_ref[...] = _rms(h1, wpre_ref[...]).astype(hn_ref.dtype)


def _post_pre(rows, xp, meta, xs, y, w_post, w_pre):
    d = xp.shape[-1]
    m = rows.blocks * RB
    row = pl.BlockSpec((RB, d), lambda r: (r, 0))
    vec = pl.BlockSpec((1, d), lambda r: (0, 0))
    return pl.pallas_call(
        functools.partial(_post_pre_kernel, rows=rows),
        grid=(rows.blocks,),
        in_specs=rows.in_specs(d) + [row, vec, vec],
        out_specs=[row, row],
        out_shape=[jax.ShapeDtypeStruct((m, d), F32), jax.ShapeDtypeStruct((m, d), BF16)],
        scratch_shapes=[pltpu.VMEM((RB, d), F32)],
        compiler_params=_params(("parallel",)),
        name="post_pre_norm",
    )(xp, meta, xs, y, w_post.reshape(1, d), w_pre.reshape(1, d))


def _post_out_kernel(h_ref, y_ref, wpost_ref, yp_ref, ys_ref, *, rows):
    _, is_seq, _, is_sample = rows.kinds(pl.program_id(0))
    out = h_ref[...] + _rms(y_ref[...], wpost_ref[...])

    @pl.when(is_seq)
    def _():
        yp_ref[0] = out

    @pl.when(is_sample)
    def _():
        ys_ref[...] = out


def _post_out(rows, h, y, w_post, seq):
    d = h.shape[1]
    row = pl.BlockSpec((RB, d), lambda r: (r, 0))
    vec = pl.BlockSpec((1, d), lambda r: (0, 0))
    return pl.pallas_call(
        functools.partial(_post_out_kernel, rows=rows),
        grid=(rows.blocks,),
        in_specs=[row, row, vec],
        out_specs=[pl.BlockSpec((1, RB, d), rows.prompt_index),
                   pl.BlockSpec((RB, d), rows.sample_index)],
        out_shape=[jax.ShapeDtypeStruct((rows.nb, seq, d), F32),
                   jax.ShapeDtypeStruct((rows.sample_blocks * RB, d), F32)],
        compiler_params=_params(("arbitrary",)),
        name="post_out",
    )(h, y, w_post.reshape(1, d))


def _accumulate(acc_ref, x_ref, w_ref, w_t=False):
    k = pl.program_id(2)
    dims = (((1,), (1 if w_t else 0,)), ((), ()))

    def prod():
        return lax.dot_general(x_ref[...], w_ref[...].astype(BF16), dims, preferred_element_type=F32)

    @pl.when(k == 0)
    def _():
        acc_ref[...] = prod()

    @pl.when(k > 0)
    def _():
        acc_ref[...] += prod()


def _mm_kernel(x_ref, w_ref, o_ref, *scratch, relu2, w_t):
    acc_ref = scratch[0] if scratch else o_ref
    _accumulate(acc_ref, x_ref, w_ref, w_t)

    if scratch:
        @pl.when(pl.program_id(2) == pl.num_programs(2) - 1)
        def _():
            acc = acc_ref[...]
            if relu2:
                acc = jnp.square(jnp.maximum(acc, 0.0))
            o_ref[...] = acc.astype(o_ref.dtype)


MM_TM = 1792
MM_TN = 1024
MM_TK = 2048


def _matmul(x, w, out_dtype, relu2=False, name="matmul", n=None, w_col=None, w_t=False, w_n0=None):
    m, kd = x.shape
    n = w.shape[0 if w_t else 1] if n is None else n
    tm = _pick(m, MM_TM, 16)
    tn = _pick(n, MM_TN, LANES)
    tk = _pick(kd, MM_TK, LANES)
    w_col = (lambda j: j) if w_col is None else w_col
    assert out_dtype == BF16 or not relu2
    if w_n0 is not None:
        assert w_t and w_n0 % SUBLANES == 0
        w_spec = pl.BlockSpec((pl.Element(tn), pl.Element(tk)),
                              lambda i, j, k: (pl.multiple_of(w_n0 + j * tn, SUBLANES),
                                               pl.multiple_of(k * tk, LANES)))
    elif w_t:
        w_spec = pl.BlockSpec((tn, tk), lambda i, j, k: (w_col(j), k))
    else:
        w_spec = pl.BlockSpec((tk, tn), lambda i, j, k: (k, w_col(j)))
    scratch = [pltpu.VMEM((tm, tn), F32)] if out_dtype != F32 else []
    return pl.pallas_call(
        functools.partial(_mm_kernel, relu2=relu2, w_t=w_t),
        grid=(m // tm, n // tn, kd // tk),
        in_specs=[pl.BlockSpec((tm, tk), lambda i, j, k: (i, k)), w_spec],
        out_specs=pl.BlockSpec((tm, tn), lambda i, j, k: (i, j)),
        out_shape=jax.ShapeDtypeStruct((m, n), out_dtype),
        scratch_shapes=scratch,
        compiler_params=_params(("parallel", "parallel", "arbitrary")),
        name=name,
    )(x, w)


def _merge_kernel(oa_ref, ob_ref, wa_ref, wb_ref, ga_ref, gb_ref, o_ref, acc_ref):
    def gated(g_ref, x_ref, w_ref):
        prod = jnp.dot(x_ref[...], w_ref[...].astype(BF16), preferred_element_type=F32)
        return jax.nn.sigmoid(g_ref[...].astype(F32)) * prod

    acc_ref[...] = gated(ga_ref, oa_ref, wa_ref)
    o_ref[...] = (acc_ref[...] + gated(gb_ref, ob_ref, wb_ref)).astype(o_ref.dtype)


def _merge(oa, ob, wa, wb, z):
    m, kd = oa.shape
    n = wa.shape[1]
    tm = _pick(m, 1280, 16)
    tn = _pick(n, 512, LANES)
    gb0 = n // tn
    act = pl.BlockSpec((tm, kd), lambda i, j: (i, 0))
    wgt = pl.BlockSpec((kd, tn), lambda i, j: (0, j))
    return pl.pallas_call(
        _merge_kernel,
        grid=(m // tm, n // tn),
        in_specs=[act, act, wgt, wgt,
                  pl.BlockSpec((tm, tn), lambda i, j: (i, j)),
                  pl.BlockSpec((tm, tn), lambda i, j: (i, gb0 + j))],
        out_specs=pl.BlockSpec((tm, tn), lambda i, j: (i, j)),
        out_shape=jax.ShapeDtypeStruct((m, n), BF16),
        scratch_shapes=[pltpu.VMEM((tm, tn), F32)],
        compiler_params=_params(("parallel", "arbitrary")),
        name="branch_merge",
    )(oa, ob, wa, wb, z, z)


def _rope_tables(pos):
    half = ROT_DIM // 2
    inv_freq = ROPE_THETA ** (-jnp.arange(half, dtype=F32) * (2.0 / ROT_DIM))
    ang = pos.astype(F32)[:, None] * inv_freq[None, :]
    cos, sin = jnp.cos(ang), jnp.sin(ang)
    rows = pos.shape[0]
    rest = A_HEAD_DIM - ROT_DIM
    one = jnp.ones((rows, rest), F32)
    zero = jnp.zeros((rows, rest), F32)
    zh = jnp.zeros((rows, half), F32)
    reps = LANES // A_HEAD_DIM
    c = jnp.tile(jnp.concatenate([cos, cos, one], axis=1), (1, reps))
    s1 = jnp.tile(jnp.concatenate([-sin, zh, zero], axis=1), (1, reps))
    s2 = jnp.tile(jnp.concatenate([zh, sin, zero], axis=1), (1, reps))
    return c, s1, s2


def _rope_kernel(q_ref, k_ref, c_ref, s1_ref, s2_ref, qo_ref, ko_ref):
    half = ROT_DIM // 2
    c, s1, s2 = c_ref[...], s1_ref[...], s2_ref[...]

    def rot(x):
        return (x * c + pltpu.roll(x, LANES - half, 1) * s1 + pltpu.roll(x, half, 1) * s2)

    scale = A_HEAD_DIM ** -0.5
    for j in range(QA_DIM // LANES):
        sl = slice(j * LANES, (j + 1) * LANES)
        qo_ref[:, sl] = (rot(q_ref[:, sl].astype(F32)) * scale).astype(qo_ref.dtype)
    for j in range(KVA_DIM // LANES):
        sl = slice(j * LANES, (j + 1) * LANES)
        ko_ref[:, sl] = rot(k_ref[:, sl].astype(F32))


def _rope(z, tables):
    m = z.shape[0]
    tr = _pick(m, 256, 16)
    tab = pl.BlockSpec((tr, LANES), lambda i: (i, 0))
    return pl.pallas_call(
        _rope_kernel,
        grid=(m // tr,),
        in_specs=[pl.BlockSpec((tr, QA_DIM), lambda i: (i, COL_QA // QA_DIM)),
                  pl.BlockSpec((tr, KVA_DIM), lambda i: (i, COL_KA // KVA_DIM)),
                  tab, tab, tab],
        out_specs=[pl.BlockSpec((tr, QA_DIM), lambda i: (i, 0)),
                   pl.BlockSpec((tr, KVA_DIM), lambda i: (i, 0))],
        out_shape=[jax.ShapeDtypeStruct((m, QA_DIM), BF16),
                   jax.ShapeDtypeStruct((m, KVA_DIM), F32)],
        compiler_params=_params(("parallel",)),
        name="rope",
    )(z, z, *tables)


def _attend_heads(q_of, k_all, v_all, mask, sinks_ref, write, rows):
    mask_g = jnp.concatenate([mask] * A_GROUP, axis=0)

    def group(g):
        cs = slice(g * A_HEAD_DIM, (g + 1) * A_HEAD_DIM)
        kh, vh = k_all[:, cs], v_all[:, cs]
        qg = jnp.concatenate([q_of(g * A_GROUP + j) for j in range(A_GROUP)], axis=0)
        sk = jnp.concatenate(
            [jnp.full((rows, 1), sinks_ref[g * A_GROUP + j], F32) for j in range(A_GROUP)], axis=0)
        s = lax.dot_general(qg, kh, (((1,), (1,)), ((), ())), preferred_element_type=F32)
        yield
        s = jnp.where(mask_g, s, NEG)
        mx = jnp.maximum(jnp.max(s, axis=-1, keepdims=True), sk)
        p = jnp.exp(s - mx)
        den = jnp.sum(p, axis=-1, keepdims=True) + jnp.exp(sk - mx)
        o = jnp.dot(p.astype(BF16), vh, preferred_element_type=F32) / den
        yield
        for j in range(A_GROUP):
            write(g * A_GROUP + j, o[j * rows:(j + 1) * rows])

    return [group(g) for g in range(A_KV_HEADS)]


def _attn_prompt_kernel(sinks_ref, q_ref, kp_ref, kc_ref, vp_ref, vc_ref, o_ref):
    i = pl.program_id(1)
    blk = WINDOW
    k_all = jnp.concatenate([kp_ref[...], kc_ref[...]], axis=0).astype(BF16)
    v_all = jnp.concatenate([vp_ref[...], vc_ref[...]], axis=0).astype(BF16)
    qrow = i * blk + lax.broadcasted_iota(jnp.int32, (blk, 2 * blk), 0)
    krow = (i - 1) * blk + lax.broadcasted_iota(jnp.int32, (blk, 2 * blk), 1)
    diff = qrow - krow
    mask = (diff >= 0) & (diff <= WINDOW) & (krow >= FRONT)

    def q_of(h):
        return q_ref[:, h * A_HEAD_DIM:(h + 1) * A_HEAD_DIM]

    def write(h, o):
        o_ref[:, h * A_HEAD_DIM:(h + 1) * A_HEAD_DIM] = o.astype(o_ref.dtype)

    _interleave(_attend_heads(q_of, k_all, v_all, mask, sinks_ref, write, blk))


def _attn_prompt(qr, kr, z, sinks, nb, lp):
    blk = WINDOW
    nblk = lp // blk
    vcol = COL_VA // KVA_DIM

    def cur(b, i):
        return b * nblk + i

    def prev(b, i):
        return b * nblk + jnp.maximum(i - 1, 0)

    return pl.pallas_call(
        _attn_prompt_kernel,
        grid=(nb, nblk),
        in_specs=[pl.BlockSpec(memory_space=pltpu.SMEM),
                  pl.BlockSpec((blk, QA_DIM), lambda b, i: (cur(b, i), 0)),
                  pl.BlockSpec((blk, KVA_DIM), lambda b, i: (prev(b, i), 0)),
                  pl.BlockSpec((blk, KVA_DIM), lambda b, i: (cur(b, i), 0)),
                  pl.BlockSpec((blk, KVA_DIM), lambda b, i: (prev(b, i), vcol)),
                  pl.BlockSpec((blk, KVA_DIM), lambda b, i: (cur(b, i), vcol))],
        out_specs=pl.BlockSpec((blk, QA_DIM), lambda b, i: (cur(b, i), 0)),
        out_shape=jax.ShapeDtypeStruct((qr.shape[0], QA_DIM), BF16),
        compiler_params=_params(("parallel", "parallel")),
        name="attn_prompt",
    )(sinks, qr, kr, kr, z, z)


SAMPLE_ROWS = 8
SAMPLE_KEYS = 256


def _attn_sample_kernel(sinks_ref, q_ref, kn_ref, vn_ref, ck_ref, cv_ref, _, o_ref, ko_ref, vo_ref,
                        kall_ref, vall_ref, o_scr, *, bb, n_buf, t_new):
    pad = SAMPLE_KEYS - n_buf - SAMPLE_ROWS
    qrow = lax.broadcasted_iota(jnp.int32, (SAMPLE_ROWS, SAMPLE_KEYS), 0)
    kcol = lax.broadcasted_iota(jnp.int32, (SAMPLE_ROWS, SAMPLE_KEYS), 1)
    diff = qrow + n_buf - kcol
    mask = (diff >= 0) & (diff <= WINDOW) & (kcol < n_buf + t_new)
    gens = []
    for b in range(bb):
        kall_ref[b, 0:n_buf, :] = ck_ref[b]
        kall_ref[b, n_buf:n_buf + SAMPLE_ROWS, :] = kn_ref[b]
        kall_ref[b, n_buf + SAMPLE_ROWS:, :] = jnp.zeros((pad, KVA_DIM), F32)
        vall_ref[b, 0:n_buf, :] = cv_ref[b]
        vall_ref[b, n_buf:n_buf + SAMPLE_ROWS, :] = vn_ref[b]
        vall_ref[b, n_buf + SAMPLE_ROWS:, :] = jnp.zeros((pad, KVA_DIM), F32)
        ko_ref[b] = kall_ref[b, t_new:t_new + n_buf, :]
        vo_ref[b] = vall_ref[b, t_new:t_new + n_buf, :]

        def q_of(h, b=b):
            return q_ref[b, :, h * A_HEAD_DIM:(h + 1) * A_HEAD_DIM].astype(BF16)

        def write(h, o, b=b):
            o_scr[b * t_new:(b + 1) * t_new, h * A_HEAD_DIM:(h + 1) * A_HEAD_DIM] = o[0:t_new]

        gens += _attend_heads(q_of, kall_ref[b].astype(BF16), vall_ref[b].astype(BF16), mask,
                              sinks_ref, write, SAMPLE_ROWS)
    _interleave(gens)
    o_ref[...] = o_scr[...].astype(o_ref.dtype)


def _attn_sample(qn, kn, vn, cache_k, cache_v, sinks, t_new, o_full, row0):
    bd, n_buf, _ = cache_k.shape
    bb = _pick(bd, 8, 1)
    rows = bb * t_new
    assert row0 % rows == 0 and rows % 16 == 0
    blk3 = lambda r, c: pl.BlockSpec((bb, r, c), lambda i: (i, 0, 0))
    return pl.pallas_call(
        functools.partial(_attn_sample_kernel, bb=bb, n_buf=n_buf, t_new=t_new),
        grid=(bd // bb,),
        in_specs=[pl.BlockSpec(memory_space=pltpu.SMEM),
                  blk3(SAMPLE_ROWS, QA_DIM), blk3(SAMPLE_ROWS, KVA_DIM), blk3(SAMPLE_ROWS, KVA_DIM),
                  blk3(n_buf, KVA_DIM), blk3(n_buf, KVA_DIM),
                  pl.BlockSpec(memory_space=pl.ANY)],
        out_specs=[pl.BlockSpec((rows, QA_DIM), lambda i: (row0 // rows + i, 0)),
                   blk3(n_buf, KVA_DIM), blk3(n_buf, KVA_DIM)],
        out_shape=[jax.ShapeDtypeStruct(o_full.shape, o_full.dtype),
                   jax.ShapeDtypeStruct((bd, n_buf, KVA_DIM), F32),
                   jax.ShapeDtypeStruct((bd, n_buf, KVA_DIM), F32)],
        scratch_shapes=[pltpu.VMEM((bb, SAMPLE_KEYS, KVA_DIM), F32),
                        pltpu.VMEM((bb, SAMPLE_KEYS, KVA_DIM), F32),
                        pltpu.VMEM((rows, QA_DIM), F32)],
        input_output_aliases={6: 0},
        compiler_params=_params(("parallel",)),
        name="attn_sample",
    )(sinks, qn, kn, vn, cache_k, cache_v, o_full)


TILE = 128
TOK = SUBLANES


def _softplus(x):
    return jnp.maximum(x, 0.0) + jnp.log(1.0 + jnp.exp(-jnp.abs(x)))


def _bf(x):
    return x.astype(BF16)


def _mm(a, b):
    return jnp.dot(_bf(a), _bf(b), preferred_element_type=F32)


def _mm_nt(a, b):
    return lax.dot_general(_bf(a), _bf(b), (((1,), (1,)), ((), ())), preferred_element_type=F32)


def _mm_tn(a, b):
    return lax.dot_general(a, b, (((0,), (0,)), ((), ())), preferred_element_type=F32)


def _tile_masks(block):
    r = lax.broadcasted_iota(jnp.int32, (TILE, TILE), 0)
    q = lax.broadcasted_iota(jnp.int32, (TILE, TILE), 1)

    def same(size):
        sh = size.bit_length() - 1
        return jnp.right_shift(r, sh) == jnp.right_shift(q, sh)

    tri = same(block) & (r >= q)
    strict = same(block) & (r > q)
    levels = []
    size = SUBLANES
    while size < block:
        levels.append(same(2 * size) & jnp.logical_not(same(size)))
        size *= 2
    return tri, strict, same(SUBLANES), levels


def _inverse_minus_eye(m, same8, levels):
    n1 = -jnp.where(same8, m, 0.0)
    n1b = _bf(n1)
    n2 = _mm(n1b, n1b)
    yield
    n2b = _bf(n2)
    n4 = _mm(n2b, n2b)
    n12 = _mm(n1b, n2b)
    yield
    a = n1 + n2 + n12
    d = a + n4 + _mm(a, n4)
    yield
    for mask in levels:
        off = jnp.where(mask, m, 0.0)
        db = _bf(d)
        p = off + _mm(off, db)
        yield
        d = d - (p + _mm(db, p))
        yield
    return d


def _delta_tile(q, k, v, gb, bb, grow, glast, masks):
    tri, strict, same8, levels = masks
    decay = jnp.where(tri, jnp.exp(jnp.where(tri, gb - grow, 0.0)), 0.0)
    eg = jnp.exp(gb)
    kb = k * bb
    aq = _mm_nt(jnp.concatenate([kb, q], axis=0), k)
    yield
    m = jnp.where(strict, aq[:TILE] * decay, 0.0)
    qk = aq[TILE:] * decay
    d = yield from _inverse_minus_eye(m, same8, levels)
    rhs = jnp.concatenate([v * bb, kb * eg], axis=1)
    sol = rhs + _mm(d, rhs)
    yield
    return sol[:, :B_HEAD_DIM], sol[:, B_HEAD_DIM:], qk, q * eg, k * jnp.exp(glast - gb)


def _interleave(gens):
    live = list(gens)
    while live:
        nxt = []
        for g in live:
            try:
                next(g)
                nxt.append(g)
            except StopIteration:
                pass
        live = nxt


def _l2(x):
    return x * lax.rsqrt(jnp.sum(x * x, axis=-1, keepdims=True) + EPS)


def _gated_norm(o, zb, dnorm):
    return _rms(o, dnorm) * (zb * jax.nn.sigmoid(zb))


def _delta_prompt_kernel(ab_ref, abt_ref, xq_ref, xk_ref, xv_ref, pq_ref, pk_ref, pv_ref, hist_ref,
                         zb_ref, s0_ref, convw_ref, arow_ref, acol_ref, dnorm_ref,
                         ob_ref, sfin_ref, s_ref, xx_ref, *, front, n_valid):
    c = pl.program_id(1)

    @pl.when(c == 0)
    def _():
        s_ref[...] = s0_ref[0]
        xx_ref[0:HIST_ROWS, :] = hist_ref[0]

    @pl.when(c > 0)
    def _():
        lo = PREV_ROWS - HIST_ROWS
        xx_ref[0:HIST_ROWS, 0:QB_DIM] = pq_ref[lo:, :].astype(F32)
        xx_ref[0:HIST_ROWS, QB_DIM:2 * QB_DIM] = pk_ref[lo:, :].astype(F32)
        xx_ref[0:HIST_ROWS, 2 * QB_DIM:] = pv_ref[lo:, :].astype(F32)

    xx_ref[HIST_ROWS:, 0:QB_DIM] = xq_ref[...].astype(F32)
    xx_ref[HIST_ROWS:, QB_DIM:2 * QB_DIM] = xk_ref[...].astype(F32)
    xx_ref[HIST_ROWS:, 2 * QB_DIM:] = xv_ref[...].astype(F32)

    pos_col = c * TILE + lax.broadcasted_iota(jnp.int32, (TILE, 1), 0)
    valid_col = ((pos_col >= front) & (pos_col < front + n_valid)).astype(F32)
    pos_row = c * TILE + lax.broadcasted_iota(jnp.int32, (1, TILE), 1)
    valid_row = ((pos_row >= front) & (pos_row < front + n_valid)).astype(F32)

    ab = ab_ref[...]
    beta = jax.nn.sigmoid(ab)
    g_col = -jnp.exp(arow_ref[0:1, :]) * _softplus(ab + arow_ref[1:2, :]) * valid_col
    g_row = -jnp.exp(acol_ref[:, 0:1]) * _softplus(abt_ref[0, 0] + acol_ref[:, 1:2]) * valid_row
    ri = lax.broadcasted_iota(jnp.int32, (TILE, TILE), 0)
    ci = lax.broadcasted_iota(jnp.int32, (TILE, TILE), 1)
    cum_col = jnp.dot((ri >= ci).astype(F32), g_col, precision=HI, preferred_element_type=F32)
    cum_row = jnp.dot(g_row, (ri <= ci).astype(F32), precision=HI, preferred_element_type=F32)
    masks = _tile_masks(TILE)
    dnorm = dnorm_ref[...]

    def conv(col):
        cs = slice(col, col + B_HEAD_DIM)
        x = xx_ref[:, cs]
        y = x[HIST_ROWS:] * convw_ref[CONV_W - 1:CONV_W, cs]
        for j in range(CONV_W - 1):
            shifted = pltpu.roll(x, CONV_W - 1 - j, 0)[HIST_ROWS:]
            y = y + shifted * convw_ref[j:j + 1, cs]
        return y * jax.nn.sigmoid(y) * valid_col

    def head(h):
        off = h * B_HEAD_DIM
        q = _l2(conv(off)) * (B_HEAD_DIM ** -0.5)
        k = _l2(conv(QB_DIM + off))
        v = conv(2 * QB_DIM + off)
        gb = jnp.broadcast_to(cum_col[:, B_HEADS + h:B_HEADS + h + 1], (TILE, LANES))
        bb = jnp.broadcast_to(beta[:, h:h + 1], (TILE, LANES))
        grow = cum_row[B_HEADS + h:B_HEADS + h + 1, :]
        glast = gb[TILE - 1:TILE, :]
        u, w, qk, qd, kd = yield from _delta_tile(q, k, v, gb, bb, grow, glast, masks)
        s = s_ref[h]
        wq = _mm(jnp.concatenate([w, qd], axis=0), s)
        yield
        v_new = u - wq[:TILE]
        o = wq[TILE:] + _mm(qk, v_new)
        s_ref[h] = s * jnp.exp(glast) + _mm_tn(kd, v_new)
        yield
        zb = zb_ref[:, off:off + B_HEAD_DIM].astype(F32)
        ob_ref[:, off:off + B_HEAD_DIM] = _gated_norm(o, zb, dnorm).astype(ob_ref.dtype)

    _interleave(head(h) for h in range(B_HEADS))

    @pl.when(c == pl.num_programs(1) - 1)
    def _():
        sfin_ref[0] = s_ref[...]


def _delta_prompt(z, ab, abt, hist, s0, conv_w, arow, acol, dnorm, *, nseq, rows_per_seq, front,
                  n_valid):
    tps = rows_per_seq // TILE
    ppt = TILE // PREV_ROWS

    def cur(b, c):
        return b * tps + c

    def prev(b, c):
        return jnp.maximum((b * tps + c) * ppt - 1, 0)

    def zcol(j):
        return pl.BlockSpec((TILE, QB_DIM), lambda b, c: (cur(b, c), j))

    def pcol(j):
        return pl.BlockSpec((PREV_ROWS, QB_DIM), lambda b, c: (prev(b, c), j))

    whole = lambda a: pl.BlockSpec(a.shape, lambda b, c: (0,) * a.ndim)
    state = pl.BlockSpec((1, B_HEADS, B_HEAD_DIM, B_HEAD_DIM), lambda b, c: (b, 0, 0, 0))
    return pl.pallas_call(
        functools.partial(_delta_prompt_kernel, front=front, n_valid=n_valid),
        grid=(nseq, tps),
        in_specs=[pl.BlockSpec((TILE, LANES), lambda b, c: (cur(b, c), 0)),
                  pl.BlockSpec((1, 1, 2 * B_HEADS, TILE), lambda b, c: (b, c, 0, 0)),
                  zcol(0), zcol(1), zcol(2), pcol(0), pcol(1), pcol(2),
                  pl.BlockSpec((1, HIST_ROWS, CONV_DIM), lambda b, c: (b, 0, 0)),
                  zcol(COL_ZB // QB_DIM), state,
                  whole(conv_w), whole(arow), whole(acol), whole(dnorm)],
        out_specs=[pl.BlockSpec((TILE, QB_DIM), lambda b, c: (cur(b, c), 0)), state],
        out_shape=[jax.ShapeDtypeStruct((z.shape[0], QB_DIM), BF16),
                   jax.ShapeDtypeStruct(s0.shape, F32)],
        scratch_shapes=[pltpu.VMEM((B_HEADS, B_HEAD_DIM, B_HEAD_DIM), F32),
                        pltpu.VMEM((HIST_ROWS + TILE, CONV_DIM), F32)],
        compiler_params=_params(("parallel", "arbitrary")),
        name="delta_prompt",
    )(ab, abt, z, z, z, z, z, z, hist, z, s0, conv_w, arow, acol, dnorm)


def _delta_sample_kernel(x_ref, ab_ref, abt_ref, hist_ref, s0_ref, convw_ref, arow_ref, alane_ref,
                         dnorm_ref, _, ob_ref, sout_ref, xx_ref, ob_scr, *, bb, t_new):
    masks = _tile_masks(TOK)
    trow = lax.broadcasted_iota(jnp.int32, (TOK, 1), 0)
    valid_t = (trow < t_new).astype(F32)
    tlane = jnp.bitwise_and(lax.broadcasted_iota(jnp.int32, (TOK, LANES), 1), TOK - 1)
    valid_lane = (tlane < t_new).astype(F32)
    dnorm = dnorm_ref[...]
    first = HIST_ROWS - (CONV_W - 1)

    def stack(a, base):
        return jnp.concatenate(
            [a[:, base + h * B_HEAD_DIM:base + (h + 1) * B_HEAD_DIM] for h in range(B_HEADS)], axis=0)

    def rows_of(col_of):
        return jnp.concatenate(
            [jnp.broadcast_to(col_of(h), (TOK, LANES)) for h in range(B_HEADS)], axis=0)

    def seq(b):
        xx_ref[0:HIST_ROWS, :] = hist_ref[b]
        xx_ref[HIST_ROWS:, :] = x_ref[b, :, 0:CONV_DIM]
        y = xx_ref[first:first + TOK, :] * convw_ref[0:1, :]
        for j in range(1, CONV_W):
            y = y + xx_ref[first + j:first + j + TOK, :] * convw_ref[j:j + 1, :]
        y = y * jax.nn.sigmoid(y) * valid_t
        q = _l2(stack(y, 0)) * (B_HEAD_DIM ** -0.5)
        k = _l2(stack(y, QB_DIM))
        v = stack(y, 2 * QB_DIM)

        ab = ab_ref[b]
        beta = jax.nn.sigmoid(ab)
        g = -jnp.exp(arow_ref[0:1, :]) * _softplus(ab + arow_ref[1:2, :]) * valid_t
        gl = (-jnp.exp(alane_ref[0:1, :]) * _softplus(abt_ref[b] + alane_ref[1:2, :])) * valid_lane
        step = 1
        while step < TOK:
            g = g + jnp.where(trow >= step, pltpu.roll(g, step, 0), 0.0)
            gl = gl + jnp.where(tlane >= step, pltpu.roll(gl, step, 1), 0.0)
            step *= 2
        gb = rows_of(lambda h: g[:, B_HEADS + h:B_HEADS + h + 1])
        bt = rows_of(lambda h: beta[:, h:h + 1])
        glast = rows_of(lambda h: g[TOK - 1:TOK, B_HEADS + h:B_HEADS + h + 1])
        u, w, qk, qd, kd = yield from _delta_tile(q, k, v, gb, bt, gl[0:1, :], glast, masks)

        wq = []
        for h in range(B_HEADS):
            rs = slice(h * TOK, (h + 1) * TOK)
            wq.append(_mm(jnp.concatenate([w[rs], qd[rs]], axis=0), s0_ref[b, h]))
        yield
        v_new = u - jnp.concatenate([x[:TOK] for x in wq], axis=0)
        o = jnp.concatenate([x[TOK:] for x in wq], axis=0) + _mm(qk, v_new)
        yield
        for h in range(B_HEADS):
            rs = slice(h * TOK, (h + 1) * TOK)
            sout_ref[b, h] = (s0_ref[b, h] * jnp.exp(glast[h * TOK:h * TOK + 1, :])
                              + _mm_tn(kd[rs], v_new[rs]))
        ob = _gated_norm(o, stack(x_ref[b], COL_ZB), dnorm)
        for h in range(B_HEADS):
            ob_scr[b * t_new:(b + 1) * t_new, h * B_HEAD_DIM:(h + 1) * B_HEAD_DIM] = (
                ob[h * TOK:h * TOK + t_new])

    _interleave(seq(b) for b in range(bb))
    ob_ref[...] = ob_scr[...].astype(ob_ref.dtype)


def _delta_sample(xs, ab, abt, hist, s0, conv_w, arow, alane, dnorm, t_new, o_full, row0):
    bd = xs.shape[0]
    bb = _pick(bd, 4, 1)
    rows = bb * t_new
    assert row0 % rows == 0 and rows % 16 == 0
    blk = lambda a: pl.BlockSpec((bb,) + a.shape[1:], lambda i: (i,) + (0,) * (a.ndim - 1))
    whole = lambda a: pl.BlockSpec(a.shape, lambda i: (0,) * a.ndim)
    return pl.pallas_call(
        functools.partial(_delta_sample_kernel, bb=bb, t_new=t_new),
        grid=(bd // bb,),
        in_specs=[blk(xs), blk(ab), blk(abt), blk(hist), blk(s0),
                  whole(conv_w), whole(arow), whole(alane), whole(dnorm),
                  pl.BlockSpec(memory_space=pl.ANY)],
        out_specs=[pl.BlockSpec((rows, QB_DIM), lambda i: (row0 // rows + i, 0)), blk(s0)],
        out_shape=[jax.ShapeDtypeStruct(o_full.shape, o_full.dtype),
                   jax.ShapeDtypeStruct(s0.shape, F32)],
        scratch_shapes=[pltpu.VMEM((HIST_ROWS + TOK, CONV_DIM), F32),
                        pltpu.VMEM((rows, QB_DIM), F32)],
        input_output_aliases={9: 0},
        compiler_params=_params(("parallel",)),
        name="delta_sample",
    )(xs, ab, abt, hist, s0, conv_w, arow, alane, dnorm, o_full)


def _layer(rows, xp, meta, xs, tables, nb, lp, n_tok, bd, t_new, cache_k, cache_v, state_conv,
           state_delta, norm_mix_pre, norm_mix_post, norm_mlp_pre, norm_mlp_post, w_in, sinks, conv_w,
           a_log, dt_bias, delta_norm, w_branch_a, w_branch_b, w_out, w_up, w_down):
    p_rows = nb * lp

    assert SRC_CONV % MM_TN == 0 and COL_QA % MM_TN == 0 and SRC_BETA % LANES == 0
    n_front, src_front = COL_QA // MM_TN, SRC_CONV // MM_TN
    w_in_t = w_in.T
    hn = _norm_in(rows, xp, meta, xs, norm_mix_pre)
    z = _matmul(hn, w_in_t, BF16, name="in_proj", n=Z_DIM, w_t=True,
                w_col=lambda j: jnp.where(j < n_front, j + src_front, j - n_front))
    ab = _matmul(hn, w_in_t, F32, name="gate_proj", n=LANES, w_t=True,
                 w_col=lambda j: SRC_BETA // LANES)
    zg = _matmul(hn, w_in_t, BF16, name="gate_in_proj", n=2 * D_MODEL, w_t=True, w_n0=SRC_GA)

    qr, kr = _rope(z, tables)
    oa_p = _attn_prompt(qr, kr, z, sinks, nb, lp)

    def sample_pad(a, c0=0, c1=None):
        a = a[p_rows:, c0:c1].astype(F32)
        return jnp.pad(a.reshape(bd, t_new, a.shape[1]), ((0, 0), (0, SAMPLE_ROWS - t_new), (0, 0)))

    n_buf = cache_k.shape[1]
    oa, k_s, v_s = _attn_sample(
        sample_pad(qr), sample_pad(kr), sample_pad(z, COL_VA, COL_VA + KVA_DIM),
        cache_k.reshape(bd, n_buf, KVA_DIM), cache_v.reshape(bd, n_buf, KVA_DIM), sinks, t_new,
        oa_p, p_rows)

    arow = jnp.zeros((2, LANES), F32)
    arow = arow.at[0, B_HEADS:2 * B_HEADS].set(a_log).at[1, B_HEADS:2 * B_HEADS].set(dt_bias)
    acol = jnp.zeros((2 * B_HEADS, LANES), F32)
    acol = acol.at[B_HEADS:, 0].set(a_log).at[B_HEADS:, 1].set(dt_bias)
    alane = jnp.stack([jnp.repeat(a_log, TOK), jnp.repeat(dt_bias, TOK)])
    dnorm = delta_norm.reshape(1, B_HEAD_DIM)
    abt_p = jnp.swapaxes(
        ab[:p_rows].reshape(nb, lp // TILE, TILE, LANES)[..., :2 * B_HEADS], 2, 3)
    ob_p, d_p = _delta_prompt(
        z, ab, abt_p, jnp.zeros((nb, HIST_ROWS, CONV_DIM), F32),
        jnp.zeros((nb, B_HEADS, B_HEAD_DIM, B_HEAD_DIM), F32),
        conv_w, arow, acol, dnorm, nseq=nb, rows_per_seq=lp, front=FRONT, n_valid=n_tok)

    x_tok = sample_pad(z, 0, COL_ZB + QB_DIM)
    ab_s = sample_pad(ab)
    abt_s = jnp.swapaxes(ab_s[..., B_HEADS:2 * B_HEADS], 1, 2).reshape(bd, 1, B_HEADS * TOK)
    hist_s = jnp.pad(state_conv, ((0, 0), (HIST_ROWS - (CONV_W - 1), 0), (0, 0)))
    ob, d_s = _delta_sample(x_tok, ab_s, abt_s, hist_s, state_delta, conv_w, arow, alane, dnorm, t_new,
                            ob_p, p_rows)

    merged = _merge(oa, ob, w_branch_a, w_branch_b, zg)
    y = _matmul(merged, w_out, F32, name="out_proj")
    h1, hn2 = _post_pre(rows, xp, meta, xs, y, norm_mix_post, norm_mlp_pre)
    u = _matmul(hn2, w_up, BF16, relu2=True, name="mlp_up")
    y2 = _matmul(u, w_down, F32, name="mlp_down")
    y_prompt, y_sample = _post_out(rows, h1, y2, norm_mlp_post, xp.shape[1])

    last = FRONT + n_tok

    def tail_rows(a, n, c0, c1):
        return jnp.stack([a[b * lp + last - n:b * lp + last, c0:c1] for b in range(nb)]).astype(F32)

    k_p = tail_rows(kr, WINDOW, 0, KVA_DIM)
    v_p = tail_rows(z, WINDOW, COL_VA, COL_VA + KVA_DIM)
    c_p = tail_rows(z, CONV_W - 1, 0, CONV_DIM)
    x_s = z[p_rows:, :CONV_DIM].astype(F32).reshape(bd, t_new, CONV_DIM)
    c_s = jnp.concatenate([state_conv, x_s], axis=1)[:, -(CONV_W - 1):]
    return y_prompt, y_sample, k_p, v_p, c_p, d_p, k_s, v_s, c_s, d_s


def kernel(x_prompt, x_sample, cache_win_k, cache_win_v, state_conv, state_delta, meta_tokens,
           norm_mix_pre, norm_mix_post, norm_mlp_pre, norm_mlp_post, w_in, sinks, conv_w,
           a_log, dt_bias, delta_norm, w_branch_a, w_branch_b, w_out, w_up, w_down):
    nb, seq, d = x_prompt.shape
    bd, t_new, _ = x_sample.shape
    depth = w_in.shape[0]
    assert depth == 1 and CONV_W - 1 <= t_new <= SAMPLE_ROWS == TOK
    n_tok = N_META + seq
    assert n_tok >= WINDOW
    lp = -(-(FRONT + n_tok) // WINDOW) * WINDOW
    rows = _Rows(nb, seq, lp, bd * t_new)

    pos_p = jnp.maximum(jnp.arange(lp) - FRONT, 0)
    pos = jnp.concatenate([jnp.tile(pos_p, nb), jnp.tile(PAST_LEN + jnp.arange(t_new), bd)])
    tables = _rope_tables(pos)

    l = 0
    y_prompt, y_sample, k_p, v_p, c_p, d_p, k_s, v_s, c_s, d_s = _layer(
        rows, x_prompt, meta_tokens, x_sample.reshape(bd * t_new, d), tables, nb, lp, n_tok, bd,
        t_new, cache_win_k[l], cache_win_v[l], state_conv[l],
        state_delta[l], norm_mix_pre[l], norm_mix_post[l], norm_mlp_pre[l], norm_mlp_post[l],
        w_in[l], sinks[l], conv_w[l], a_log[l], dt_bias[l], delta_norm[l], w_branch_a[l],
        w_branch_b[l], w_out[l], w_up[l], w_down[l])

    y_sample = y_sample.reshape(bd, t_new, d)
    n_buf = cache_win_k.shape[2]
    kv = lambda a, n, r: a.reshape(1, n, r, A_KV_HEADS, A_HEAD_DIM)
    return (y_prompt, y_sample,
            kv(k_p, nb, WINDOW), kv(v_p, nb, WINDOW), c_p[None], d_p[None],
            kv(k_s, bd, n_buf), kv(v_s, bd, n_buf), c_s[None], d_s[None])
```

```python
import functools

import jax
import jax.numpy as jnp
from jax import lax
from jax.experimental import pallas as pl
from jax.experimental.pallas import tpu as pltpu

D_MODEL = 4096
PAST_LEN = 8192
N_META = 16
WINDOW = 128
A_HEADS = 32
A_KV_HEADS = 8
A_GROUP = A_HEADS // A_KV_HEADS
A_HEAD_DIM = 64
ROT_DIM = A_HEAD_DIM // 4
ROPE_THETA = 500000.0
B_HEADS = 16
B_HEAD_DIM = 128
CONV_W = 4
CHUNK = 64
D_FF = 4 * D_MODEL
EPS = 1e-6

QA_DIM = A_HEADS * A_HEAD_DIM
KVA_DIM = A_KV_HEADS * A_HEAD_DIM
QB_DIM = B_HEADS * B_HEAD_DIM
CONV_DIM = 3 * QB_DIM
FRONT = (-N_META) % CHUNK

COL_CONV = 0
COL_ZB = COL_CONV + CONV_DIM
COL_QA = COL_ZB + QB_DIM
COL_KA = COL_QA + QA_DIM
COL_VA = COL_KA + KVA_DIM
Z_DIM = COL_VA + KVA_DIM
SRC_QA = 0
SRC_KA = SRC_QA + QA_DIM
SRC_VA = SRC_KA + KVA_DIM
SRC_CONV = SRC_VA + KVA_DIM
SRC_ZB = SRC_CONV + CONV_DIM
SRC_BETA = SRC_ZB + QB_DIM
SRC_ALPHA = SRC_BETA + B_HEADS
SRC_GA = SRC_ALPHA + B_HEADS
SRC_GB = SRC_GA + D_MODEL
LANES = 128
SUBLANES = 8
RB = FRONT + N_META
HIST_ROWS = 8
PREV_ROWS = 16
NEG = -1e30
VMEM_LIMIT = 56 * 1024 * 1024
HI = lax.Precision.HIGHEST
F32 = jnp.float32
BF16 = jnp.bfloat16


def _pick(n, target, align):
    best = None
    for t in range(align, min(n, target) + 1, align):
        if n % t == 0:
            best = t
    return best if best is not None else n


def _params(sem):
    return pltpu.CompilerParams(dimension_semantics=sem, vmem_limit_bytes=VMEM_LIMIT)


def _rms(x, w):
    return x * lax.rsqrt(jnp.mean(x * x, axis=-1, keepdims=True) + EPS) * w


class _Rows:
    def __init__(self, nb, seq, lp, n_sample):
        assert FRONT + N_META == RB and seq % RB == 0 and lp % RB == 0 and n_sample % RB == 0
        self.nb, self.bps, self.seq_blocks = nb, lp // RB, seq // RB
        self.sample_blocks = n_sample // RB
        self.blocks = nb * self.bps + self.sample_blocks

    def prompt_index(self, r):
        b = jnp.minimum(r // self.bps, self.nb - 1)
        s = jnp.where(r < self.nb * self.bps, jnp.clip(r % self.bps - 1, 0, self.seq_blocks - 1),
                      self.seq_blocks - 1)
        return b, s, 0

    def sample_index(self, r):
        return jnp.clip(r - self.nb * self.bps, 0, self.sample_blocks - 1), 0

    def kinds(self, r):
        rb = r % self.bps
        is_p = r < self.nb * self.bps
        return (is_p & (rb == 0), is_p & (rb >= 1) & (rb <= self.seq_blocks),
                is_p & (rb > self.seq_blocks), jnp.logical_not(is_p))

    def in_specs(self, d, g):
        assert (self.nb * self.bps) % g == 0 and self.sample_blocks % g == 0
        first_sample, last_sample = self.nb * self.bps // g, self.sample_blocks // g - 1
        prompt = [pl.BlockSpec((1, RB, d), functools.partial(self._prompt_sub, g=g, s=s))
                  for s in range(g)]
        return prompt + [pl.BlockSpec((N_META, d), lambda r: (0, 0)),
                         pl.BlockSpec((g * RB, d),
                                      lambda r: (jnp.clip(r - first_sample, 0, last_sample), 0))]

    def _prompt_sub(self, r, *, g, s):
        return self.prompt_index(g * r + s)

    def gather(self, r, xp_refs, meta_ref, xs_ref, h_ref):
        g = len(xp_refs)
        for s, xp_ref in enumerate(xp_refs):
            is_meta, is_seq, is_zero, is_sample = self.kinds(g * r + s)
            lo = s * RB

            @pl.when(is_meta)
            def _(lo=lo):
                h_ref[lo:lo + FRONT, :] = jnp.zeros((FRONT, h_ref.shape[1]), F32)
                h_ref[lo + FRONT:lo + RB, :] = meta_ref[...]

            @pl.when(is_seq)
            def _(lo=lo, xp_ref=xp_ref):
                h_ref[lo:lo + RB, :] = xp_ref[0]

            @pl.when(is_zero)
            def _(lo=lo):
                h_ref[lo:lo + RB, :] = jnp.zeros((RB, h_ref.shape[1]), F32)

            @pl.when(is_sample)
            def _(lo=lo):
                h_ref[lo:lo + RB, :] = xs_ref[lo:lo + RB, :]


ROW_GROUP = 4


def _norm_in_kernel(*refs, rows):
    *xp_refs, meta_ref, xs_ref, w_ref, o_ref, h_ref = refs
    rows.gather(pl.program_id(0), xp_refs, meta_ref, xs_ref, h_ref)
    o_ref[...] = _rms(h_ref[...], w_ref[...]).astype(o_ref.dtype)


def _norm_in(rows, xp, meta, xs, w):
    d = xp.shape[-1]
    g = ROW_GROUP
    row = pl.BlockSpec((g * RB, d), lambda r: (r, 0))
    vec = pl.BlockSpec((1, d), lambda r: (0, 0))
    return pl.pallas_call(
        functools.partial(_norm_in_kernel, rows=rows),
        grid=(rows.blocks // g,),
        in_specs=rows.in_specs(d, g) + [vec],
        out_specs=row,
        out_shape=jax.ShapeDtypeStruct((rows.blocks * RB, d), BF16),
        scratch_shapes=[pltpu.VMEM((g * RB, d), F32)],
        compiler_params=_params(("parallel",)),
        name="norm_in",
    )(*([xp] * g), meta, xs, w.reshape(1, d))


def _post_pre_kernel(*refs, rows):
    *xp_refs, meta_ref, xs_ref, y_ref, wpost_ref, wpre_ref, h1_ref, hn_ref, h_ref = refs
    rows.gather(pl.program_id(0), xp_refs, meta_ref, xs_ref, h_ref)
    h1 = h_ref[...] + _rms(y_ref[...], wpost_ref[...])
    h1_ref[...] = h1
    hn_ref[...] = _rms(h1, wpre_ref[...]).astype(hn_ref.dtype)


def _post_pre(rows, xp, meta, xs, y, w_post, w_pre):
    d = xp.shape[-1]
    g = ROW_GROUP
    m = rows.blocks * RB
    row = pl.BlockSpec((g * RB, d), lambda r: (r, 0))
    vec = pl.BlockSpec((1, d), lambda r: (0, 0))
    return pl.pallas_call(
        functools.partial(_post_pre_kernel, rows=rows),
        grid=(rows.blocks // g,),
        in_specs=rows.in_specs(d, g) + [row, vec, vec],
        out_specs=[row, row],
        out_shape=[jax.ShapeDtypeStruct((m, d), F32), jax.ShapeDtypeStruct((m, d), BF16)],
        scratch_shapes=[pltpu.VMEM((g * RB, d), F32)],
        compiler_params=_params(("parallel",)),
        name="post_pre_norm",
    )(*([xp] * g), meta, xs, y, w_post.reshape(1, d), w_pre.reshape(1, d))


def _post_out_kernel(h_ref, y_ref, wpost_ref, yp_ref, ys_ref, *, rows):
    _, is_seq, _, is_sample = rows.kinds(pl.program_id(0))
    out = h_ref[...] + _rms(y_ref[...], wpost_ref[...])

    @pl.when(is_seq)
    def _():
        yp_ref[0] = out

    @pl.when(is_sample)
    def _():
        ys_ref[...] = out


def _post_out(rows, h, y, w_post, seq):
    d = h.shape[1]
    row = pl.BlockSpec((RB, d), lambda r: (r, 0))
    vec = pl.BlockSpec((1, d), lambda r: (0, 0))
    return pl.pallas_call(
        functools.partial(_post_out_kernel, rows=rows),
        grid=(rows.blocks,),
        in_specs=[row, row, vec],
        out_specs=[pl.BlockSpec((1, RB, d), rows.prompt_index),
                   pl.BlockSpec((RB, d), rows.sample_index)],
        out_shape=[jax.ShapeDtypeStruct((rows.nb, seq, d), F32),
                   jax.ShapeDtypeStruct((rows.sample_blocks * RB, d), F32)],
        compiler_params=_params(("arbitrary",)),
        name="post_out",
    )(h, y, w_post.reshape(1, d))


def _accumulate(acc_ref, x_ref, w_ref, w_t=False):
    k = pl.program_id(2)
    dims = (((1,), (1 if w_t else 0,)), ((), ()))

    def prod():
        return lax.dot_general(x_ref[...], w_ref[...].astype(BF16), dims, preferred_element_type=F32)

    @pl.when(k == 0)
    def _():
        acc_ref[...] = prod()

    @pl.when(k > 0)
    def _():
        acc_ref[...] += prod()


def _mm_kernel(x_ref, w_ref, o_ref, *scratch, relu2, w_t):
    acc_ref = scratch[0] if scratch else o_ref
    _accumulate(acc_ref, x_ref, w_ref, w_t)

    if scratch:
        @pl.when(pl.program_id(2) == pl.num_programs(2) - 1)
        def _():
            acc = acc_ref[...]
            if relu2:
                acc = jnp.square(jnp.maximum(acc, 0.0))
            o_ref[...] = acc.astype(o_ref.dtype)


MM_TM = 1792
MM_TN = 1024
MM_TK = 2048


def _matmul(x, w, out_dtype, relu2=False, name="matmul", n=None, w_col=None, w_t=False, w_n0=None):
    m, kd = x.shape
    n = w.shape[0 if w_t else 1] if n is None else n
    tm = _pick(m, MM_TM, 16)
    tn = _pick(n, MM_TN, LANES)
    tk = _pick(kd, MM_TK, LANES)
    w_col = (lambda j: j) if w_col is None else w_col
    assert out_dtype == BF16 or not relu2
    if w_n0 is not None:
        assert w_t and w_n0 % SUBLANES == 0
        w_spec = pl.BlockSpec((pl.Element(tn), pl.Element(tk)),
                              lambda i, j, k: (pl.multiple_of(w_n0 + j * tn, SUBLANES),
                                               pl.multiple_of(k * tk, LANES)))
    elif w_t:
        w_spec = pl.BlockSpec((tn, tk), lambda i, j, k: (w_col(j), k))
    else:
        w_spec = pl.BlockSpec((tk, tn), lambda i, j, k: (k, w_col(j)))
    scratch = [pltpu.VMEM((tm, tn), F32)] if out_dtype != F32 else []
    return pl.pallas_call(
        functools.partial(_mm_kernel, relu2=relu2, w_t=w_t),
        grid=(m // tm, n // tn, kd // tk),
        in_specs=[pl.BlockSpec((tm, tk), lambda i, j, k: (i, k)), w_spec],
        out_specs=pl.BlockSpec((tm, tn), lambda i, j, k: (i, j)),
        out_shape=jax.ShapeDtypeStruct((m, n), out_dtype),
        scratch_shapes=scratch,
        compiler_params=_params(("parallel", "parallel", "arbitrary")),
        name=name,
    )(x, w)


def _merge_kernel(oa_ref, ob_ref, wa_ref, wb_ref, ga_ref, gb_ref, o_ref, acc_ref):
    def gated(g_ref, x_ref, w_ref):
        prod = jnp.dot(x_ref[...], w_ref[...].astype(BF16), preferred_element_type=F32)
        return jax.nn.sigmoid(g_ref[...].astype(F32)) * prod

    acc_ref[...] = gated(ga_ref, oa_ref, wa_ref)
    o_ref[...] = (acc_ref[...] + gated(gb_ref, ob_ref, wb_ref)).astype(o_ref.dtype)


def _merge(oa, ob, wa, wb, z):
    m, kd = oa.shape
    n = wa.shape[1]
    tm = _pick(m, 1280, 16)
    tn = _pick(n, 512, LANES)
    gb0 = n // tn
    act = pl.BlockSpec((tm, kd), lambda i, j: (i, 0))
    wgt = pl.BlockSpec((kd, tn), lambda i, j: (0, j))
    return pl.pallas_call(
        _merge_kernel,
        grid=(m // tm, n // tn),
        in_specs=[act, act, wgt, wgt,
                  pl.BlockSpec((tm, tn), lambda i, j: (i, j)),
                  pl.BlockSpec((tm, tn), lambda i, j: (i, gb0 + j))],
        out_specs=pl.BlockSpec((tm, tn), lambda i, j: (i, j)),
        out_shape=jax.ShapeDtypeStruct((m, n), BF16),
        scratch_shapes=[pltpu.VMEM((tm, tn), F32)],
        compiler_params=_params(("parallel", "arbitrary")),
        name="branch_merge",
    )(oa, ob, wa, wb, z, z)


def _rope_tables(pos):
    half = ROT_DIM // 2
    inv_freq = ROPE_THETA ** (-jnp.arange(half, dtype=F32) * (2.0 / ROT_DIM))
    ang = pos.astype(F32)[:, None] * inv_freq[None, :]
    cos, sin = jnp.cos(ang), jnp.sin(ang)
    rows = pos.shape[0]
    rest = A_HEAD_DIM - ROT_DIM
    one = jnp.ones((rows, rest), F32)
    zero = jnp.zeros((rows, rest), F32)
    zh = jnp.zeros((rows, half), F32)
    reps = LANES // A_HEAD_DIM
    c = jnp.tile(jnp.concatenate([cos, cos, one], axis=1), (1, reps))
    s1 = jnp.tile(jnp.concatenate([-sin, zh, zero], axis=1), (1, reps))
    s2 = jnp.tile(jnp.concatenate([zh, sin, zero], axis=1), (1, reps))
    return c, s1, s2


def _rope_kernel(q_ref, k_ref, c_ref, s1_ref, s2_ref, qo_ref, ko_ref):
    half = ROT_DIM // 2
    c, s1, s2 = c_ref[...], s1_ref[...], s2_ref[...]

    def rot(x):
        return (x * c + pltpu.roll(x, LANES - half, 1) * s1 + pltpu.roll(x, half, 1) * s2)

    scale = A_HEAD_DIM ** -0.5
    for j in range(QA_DIM // LANES):
        sl = slice(j * LANES, (j + 1) * LANES)
        qo_ref[:, sl] = (rot(q_ref[:, sl].astype(F32)) * scale).astype(qo_ref.dtype)
    for j in range(KVA_DIM // LANES):
        sl = slice(j * LANES, (j + 1) * LANES)
        ko_ref[:, sl] = rot(k_ref[:, sl].astype(F32))


def _rope(z, tables):
    m = z.shape[0]
    tr = _pick(m, 256, 16)
    tab = pl.BlockSpec((tr, LANES), lambda i: (i, 0))
    return pl.pallas_call(
        _rope_kernel,
        grid=(m // tr,),
        in_specs=[pl.BlockSpec((tr, QA_DIM), lambda i: (i, COL_QA // QA_DIM)),
                  pl.BlockSpec((tr, KVA_DIM), lambda i: (i, COL_KA // KVA_DIM)),
                  tab, tab, tab],
        out_specs=[pl.BlockSpec((tr, QA_DIM), lambda i: (i, 0)),
                   pl.BlockSpec((tr, KVA_DIM), lambda i: (i, 0))],
        out_shape=[jax.ShapeDtypeStruct((m, QA_DIM), BF16),
                   jax.ShapeDtypeStruct((m, KVA_DIM), F32)],
        compiler_params=_params(("parallel",)),
        name="rope",
    )(z, z, *tables)


def _attend_heads(q_of, k_all, v_all, mask, sinks_ref, write, rows):
    mask_g = jnp.concatenate([mask] * A_GROUP, axis=0)

    def group(g):
        cs = slice(g * A_HEAD_DIM, (g + 1) * A_HEAD_DIM)
        kh, vh = k_all[:, cs], v_all[:, cs]
        qg = jnp.concatenate([q_of(g * A_GROUP + j) for j in range(A_GROUP)], axis=0)
        sk = jnp.concatenate(
            [jnp.full((rows, 1), sinks_ref[g * A_GROUP + j], F32) for j in range(A_GROUP)], axis=0)
        s = lax.dot_general(qg, kh, (((1,), (1,)), ((), ())), preferred_element_type=F32)
        yield
        s = jnp.where(mask_g, s, NEG)
        mx = jnp.maximum(jnp.max(s, axis=-1, keepdims=True), sk)
        p = jnp.exp(s - mx)
        den = jnp.sum(p, axis=-1, keepdims=True) + jnp.exp(sk - mx)
        o = jnp.dot(p.astype(BF16), vh, preferred_element_type=F32) / den
        yield
        for j in range(A_GROUP):
            write(g * A_GROUP + j, o[j * rows:(j + 1) * rows])

    return [group(g) for g in range(A_KV_HEADS)]


def _attn_prompt_kernel(sinks_ref, q_ref, kp_ref, kc_ref, vp_ref, vc_ref, o_ref):
    i = pl.program_id(1)
    blk = WINDOW
    k_all = jnp.concatenate([kp_ref[...], kc_ref[...]], axis=0).astype(BF16)
    v_all = jnp.concatenate([vp_ref[...], vc_ref[...]], axis=0).astype(BF16)
    qrow = i * blk + lax.broadcasted_iota(jnp.int32, (blk, 2 * blk), 0)
    krow = (i - 1) * blk + lax.broadcasted_iota(jnp.int32, (blk, 2 * blk), 1)
    diff = qrow - krow
    mask = (diff >= 0) & (diff <= WINDOW) & (krow >= FRONT)

    def q_of(h):
        return q_ref[:, h * A_HEAD_DIM:(h + 1) * A_HEAD_DIM]

    def write(h, o):
        o_ref[:, h * A_HEAD_DIM:(h + 1) * A_HEAD_DIM] = o.astype(o_ref.dtype)

    _interleave(_attend_heads(q_of, k_all, v_all, mask, sinks_ref, write, blk))


def _attn_prompt(qr, kr, z, sinks, nb, lp):
    blk = WINDOW
    nblk = lp // blk
    vcol = COL_VA // KVA_DIM

    def cur(b, i):
        return b * nblk + i

    def prev(b, i):
        return b * nblk + jnp.maximum(i - 1, 0)

    return pl.pallas_call(
        _attn_prompt_kernel,
        grid=(nb, nblk),
        in_specs=[pl.BlockSpec(memory_space=pltpu.SMEM),
                  pl.BlockSpec((blk, QA_DIM), lambda b, i: (cur(b, i), 0)),
                  pl.BlockSpec((blk, KVA_DIM), lambda b, i: (prev(b, i), 0)),
                  pl.BlockSpec((blk, KVA_DIM), lambda b, i: (cur(b, i), 0)),
                  pl.BlockSpec((blk, KVA_DIM), lambda b, i: (prev(b, i), vcol)),
                  pl.BlockSpec((blk, KVA_DIM), lambda b, i: (cur(b, i), vcol))],
        out_specs=pl.BlockSpec((blk, QA_DIM), lambda b, i: (cur(b, i), 0)),
        out_shape=jax.ShapeDtypeStruct((qr.shape[0], QA_DIM), BF16),
        compiler_params=_params(("parallel", "parallel")),
        name="attn_prompt",
    )(sinks, qr, kr, kr, z, z)


SAMPLE_ROWS = 8
SAMPLE_KEYS = 256


def _attn_sample_kernel(sinks_ref, q_ref, kn_ref, vn_ref, ck_ref, cv_ref, _, o_ref, ko_ref, vo_ref,
                        kall_ref, vall_ref, o_scr, *, bb, n_buf, t_new):
    pad = SAMPLE_KEYS - n_buf - SAMPLE_ROWS
    qrow = lax.broadcasted_iota(jnp.int32, (SAMPLE_ROWS, SAMPLE_KEYS), 0)
    kcol = lax.broadcasted_iota(jnp.int32, (SAMPLE_ROWS, SAMPLE_KEYS), 1)
    diff = qrow + n_buf - kcol
    mask = (diff >= 0) & (diff <= WINDOW) & (kcol < n_buf + t_new)
    gens = []
    for b in range(bb):
        kall_ref[b, 0:n_buf, :] = ck_ref[b]
        kall_ref[b, n_buf:n_buf + SAMPLE_ROWS, :] = kn_ref[b]
        kall_ref[b, n_buf + SAMPLE_ROWS:, :] = jnp.zeros((pad, KVA_DIM), F32)
        vall_ref[b, 0:n_buf, :] = cv_ref[b]
        vall_ref[b, n_buf:n_buf + SAMPLE_ROWS, :] = vn_ref[b]
        vall_ref[b, n_buf + SAMPLE_ROWS:, :] = jnp.zeros((pad, KVA_DIM), F32)
        ko_ref[b] = kall_ref[b, t_new:t_new + n_buf, :]
        vo_ref[b] = vall_ref[b, t_new:t_new + n_buf, :]

        def q_of(h, b=b):
            return q_ref[b, :, h * A_HEAD_DIM:(h + 1) * A_HEAD_DIM].astype(BF16)

        def write(h, o, b=b):
            o_scr[b * t_new:(b + 1) * t_new, h * A_HEAD_DIM:(h + 1) * A_HEAD_DIM] = o[0:t_new]

        gens += _attend_heads(q_of, kall_ref[b].astype(BF16), vall_ref[b].astype(BF16), mask,
                              sinks_ref, write, SAMPLE_ROWS)
    _interleave(gens)
    o_ref[...] = o_scr[...].astype(o_ref.dtype)


def _attn_sample(qn, kn, vn, cache_k, cache_v, sinks, t_new, o_full, row0):
    bd, n_buf, _ = cache_k.shape
    bb = _pick(bd, 8, 1)
    rows = bb * t_new
    assert row0 % rows == 0 and rows % 16 == 0
    blk3 = lambda r, c: pl.BlockSpec((bb, r, c), lambda i: (i, 0, 0))
    return pl.pallas_call(
        functools.partial(_attn_sample_kernel, bb=bb, n_buf=n_buf, t_new=t_new),
        grid=(bd // bb,),
        in_specs=[pl.BlockSpec(memory_space=pltpu.SMEM),
                  blk3(SAMPLE_ROWS, QA_DIM), blk3(SAMPLE_ROWS, KVA_DIM), blk3(SAMPLE_ROWS, KVA_DIM),
                  blk3(n_buf, KVA_DIM), blk3(n_buf, KVA_DIM),
                  pl.BlockSpec(memory_space=pl.ANY)],
        out_specs=[pl.BlockSpec((rows, QA_DIM), lambda i: (row0 // rows + i, 0)),
                   blk3(n_buf, KVA_DIM), blk3(n_buf, KVA_DIM)],
        out_shape=[jax.ShapeDtypeStruct(o_full.shape, o_full.dtype),
                   jax.ShapeDtypeStruct((bd, n_buf, KVA_DIM), F32),
                   jax.ShapeDtypeStruct((bd, n_buf, KVA_DIM), F32)],
        scratch_shapes=[pltpu.VMEM((bb, SAMPLE_KEYS, KVA_DIM), F32),
                        pltpu.VMEM((bb, SAMPLE_KEYS, KVA_DIM), F32),
                        pltpu.VMEM((rows, QA_DIM), F32)],
        input_output_aliases={6: 0},
        compiler_params=_params(("parallel",)),
        name="attn_sample",
    )(sinks, qn, kn, vn, cache_k, cache_v, o_full)


TILE = 128
TOK = SUBLANES


def _softplus(x):
    return jnp.maximum(x, 0.0) + jnp.log(1.0 + jnp.exp(-jnp.abs(x)))


def _bf(x):
    return x.astype(BF16)


def _mm(a, b):
    return jnp.dot(_bf(a), _bf(b), preferred_element_type=F32)


def _mm_nt(a, b):
    return lax.dot_general(_bf(a), _bf(b), (((1,), (1,)), ((), ())), preferred_element_type=F32)


def _mm_tn(a, b):
    return lax.dot_general(a, b, (((0,), (0,)), ((), ())), preferred_element_type=F32)


def _tile_masks(block):
    r = lax.broadcasted_iota(jnp.int32, (TILE, TILE), 0)
    q = lax.broadcasted_iota(jnp.int32, (TILE, TILE), 1)

    def same(size):
        sh = size.bit_length() - 1
        return jnp.right_shift(r, sh) == jnp.right_shift(q, sh)

    tri = same(block) & (r >= q)
    strict = same(block) & (r > q)
    levels = []
    size = SUBLANES
    while size < block:
        levels.append(same(2 * size) & jnp.logical_not(same(size)))
        size *= 2
    return tri, strict, same(SUBLANES), levels


def _inverse_minus_eye(m, same8, levels):
    n1 = -jnp.where(same8, m, 0.0)
    n1b = _bf(n1)
    n2 = _mm(n1b, n1b)
    yield
    n2b = _bf(n2)
    n4 = _mm(n2b, n2b)
    n12 = _mm(n1b, n2b)
    yield
    a = n1 + n2 + n12
    d = a + n4 + _mm(a, n4)
    yield
    for mask in levels:
        off = jnp.where(mask, m, 0.0)
        db = _bf(d)
        p = off + _mm(off, db)
        yield
        d = d - (p + _mm(db, p))
        yield
    return d


def _delta_tile(q, k, v, gb, bb, grow, glast, masks):
    tri, strict, same8, levels = masks
    decay = jnp.where(tri, jnp.exp(jnp.where(tri, gb - grow, 0.0)), 0.0)
    eg = jnp.exp(gb)
    kb = k * bb
    aq = _mm_nt(jnp.concatenate([kb, q], axis=0), k)
    yield
    m = jnp.where(strict, aq[:TILE] * decay, 0.0)
    qk = aq[TILE:] * decay
    d = yield from _inverse_minus_eye(m, same8, levels)
    rhs = jnp.concatenate([v * bb, kb * eg], axis=1)
    sol = rhs + _mm(d, rhs)
    yield
    return sol[:, :B_HEAD_DIM], sol[:, B_HEAD_DIM:], qk, q * eg, k * jnp.exp(glast - gb)


def _interleave(gens):
    live = list(gens)
    while live:
        nxt = []
        for g in live:
            try:
                next(g)
                nxt.append(g)
            except StopIteration:
                pass
        live = nxt


def _l2(x):
    return x * lax.rsqrt(jnp.sum(x * x, axis=-1, keepdims=True) + EPS)


def _gated_norm(o, zb, dnorm):
    return _rms(o, dnorm) * (zb * jax.nn.sigmoid(zb))


def _delta_prompt_kernel(ab_ref, abt_ref, xq_ref, xk_ref, xv_ref, pq_ref, pk_ref, pv_ref, hist_ref,
                         zb_ref, s0_ref, convw_ref, arow_ref, acol_ref, dnorm_ref,
                         ob_ref, sfin_ref, s_ref, xx_ref, *, front, n_valid):
    c = pl.program_id(1)

    @pl.when(c == 0)
    def _():
        s_ref[...] = s0_ref[0]
        xx_ref[0:HIST_ROWS, :] = hist_ref[0]

    @pl.when(c > 0)
    def _():
        lo = PREV_ROWS - HIST_ROWS
        xx_ref[0:HIST_ROWS, 0:QB_DIM] = pq_ref[lo:, :].astype(F32)
        xx_ref[0:HIST_ROWS, QB_DIM:2 * QB_DIM] = pk_ref[lo:, :].astype(F32)
        xx_ref[0:HIST_ROWS, 2 * QB_DIM:] = pv_ref[lo:, :].astype(F32)

    xx_ref[HIST_ROWS:, 0:QB_DIM] = xq_ref[...].astype(F32)
    xx_ref[HIST_ROWS:, QB_DIM:2 * QB_DIM] = xk_ref[...].astype(F32)
    xx_ref[HIST_ROWS:, 2 * QB_DIM:] = xv_ref[...].astype(F32)

    pos_col = c * TILE + lax.broadcasted_iota(jnp.int32, (TILE, 1), 0)
    valid_col = ((pos_col >= front) & (pos_col < front + n_valid)).astype(F32)
    pos_row = c * TILE + lax.broadcasted_iota(jnp.int32, (1, TILE), 1)
    valid_row = ((pos_row >= front) & (pos_row < front + n_valid)).astype(F32)

    ab = ab_ref[...]
    beta = jax.nn.sigmoid(ab)
    g_col = -jnp.exp(arow_ref[0:1, :]) * _softplus(ab + arow_ref[1:2, :]) * valid_col
    g_row = -jnp.exp(acol_ref[:, 0:1]) * _softplus(abt_ref[0, 0] + acol_ref[:, 1:2]) * valid_row
    ri = lax.broadcasted_iota(jnp.int32, (TILE, TILE), 0)
    ci = lax.broadcasted_iota(jnp.int32, (TILE, TILE), 1)
    cum_col = jnp.dot((ri >= ci).astype(F32), g_col, precision=HI, preferred_element_type=F32)
    cum_row = jnp.dot(g_row, (ri <= ci).astype(F32), precision=HI, preferred_element_type=F32)
    masks = _tile_masks(TILE)
    dnorm = dnorm_ref[...]

    def conv(col):
        cs = slice(col, col + B_HEAD_DIM)
        x = xx_ref[:, cs]
        y = x[HIST_ROWS:] * convw_ref[CONV_W - 1:CONV_W, cs]
        for j in range(CONV_W - 1):
            shifted = pltpu.roll(x, CONV_W - 1 - j, 0)[HIST_ROWS:]
            y = y + shifted * convw_ref[j:j + 1, cs]
        return y * jax.nn.sigmoid(y) * valid_col

    def head(h):
        off = h * B_HEAD_DIM
        q = _l2(conv(off)) * (B_HEAD_DIM ** -0.5)
        k = _l2(conv(QB_DIM + off))
        v = conv(2 * QB_DIM + off)
        gb = jnp.broadcast_to(cum_col[:, B_HEADS + h:B_HEADS + h + 1], (TILE, LANES))
        bb = jnp.broadcast_to(beta[:, h:h + 1], (TILE, LANES))
        grow = cum_row[B_HEADS + h:B_HEADS + h + 1, :]
        glast = gb[TILE - 1:TILE, :]
        u, w, qk, qd, kd = yield from _delta_tile(q, k, v, gb, bb, grow, glast, masks)
        s = s_ref[h]
        wq = _mm(jnp.concatenate([w, qd], axis=0), s)
        yield
        v_new = u - wq[:TILE]
        o = wq[TILE:] + _mm(qk, v_new)
        s_ref[h] = s * jnp.exp(glast) + _mm_tn(kd, v_new)
        yield
        zb = zb_ref[:, off:off + B_HEAD_DIM].astype(F32)
        ob_ref[:, off:off + B_HEAD_DIM] = _gated_norm(o, zb, dnorm).astype(ob_ref.dtype)

    _interleave(head(h) for h in range(B_HEADS))

    @pl.when(c == pl.num_programs(1) - 1)
    def _():
        sfin_ref[0] = s_ref[...]


def _delta_prompt(z, ab, abt, hist, s0, conv_w, arow, acol, dnorm, *, nseq, rows_per_seq, front,
                  n_valid):
    tps = rows_per_seq // TILE
    ppt = TILE // PREV_ROWS

    def cur(b, c):
        return b * tps + c

    def prev(b, c):
        return jnp.maximum((b * tps + c) * ppt - 1, 0)

    def zcol(j):
        return pl.BlockSpec((TILE, QB_DIM), lambda b, c: (cur(b, c), j))

    def pcol(j):
        return pl.BlockSpec((PREV_ROWS, QB_DIM), lambda b, c: (prev(b, c), j))

    whole = lambda a: pl.BlockSpec(a.shape, lambda b, c: (0,) * a.ndim)
    state = pl.BlockSpec((1, B_HEADS, B_HEAD_DIM, B_HEAD_DIM), lambda b, c: (b, 0, 0, 0))
    return pl.pallas_call(
        functools.partial(_delta_prompt_kernel, front=front, n_valid=n_valid),
        grid=(nseq, tps),
        in_specs=[pl.BlockSpec((TILE, LANES), lambda b, c: (cur(b, c), 0)),
                  pl.BlockSpec((1, 1, 2 * B_HEADS, TILE), lambda b, c: (b, c, 0, 0)),
                  zcol(0), zcol(1), zcol(2), pcol(0), pcol(1), pcol(2),
                  pl.BlockSpec((1, HIST_ROWS, CONV_DIM), lambda b, c: (b, 0, 0)),
                  zcol(COL_ZB // QB_DIM), state,
                  whole(conv_w), whole(arow), whole(acol), whole(dnorm)],
        out_specs=[pl.BlockSpec((TILE, QB_DIM), lambda b, c: (cur(b, c), 0)), state],
        out_shape=[jax.ShapeDtypeStruct((z.shape[0], QB_DIM), BF16),
                   jax.ShapeDtypeStruct(s0.shape, F32)],
        scratch_shapes=[pltpu.VMEM((B_HEADS, B_HEAD_DIM, B_HEAD_DIM), F32),
                        pltpu.VMEM((HIST_ROWS + TILE, CONV_DIM), F32)],
        compiler_params=_params(("parallel", "arbitrary")),
        name="delta_prompt",
    )(ab, abt, z, z, z, z, z, z, hist, z, s0, conv_w, arow, acol, dnorm)


def _delta_sample_kernel(x_ref, ab_ref, abt_ref, hist_ref, s0_ref, convw_ref, arow_ref, alane_ref,
                         dnorm_ref, _, ob_ref, sout_ref, xx_ref, ob_scr, *, bb, t_new):
    masks = _tile_masks(TOK)
    trow = lax.broadcasted_iota(jnp.int32, (TOK, 1), 0)
    valid_t = (trow < t_new).astype(F32)
    tlane = jnp.bitwise_and(lax.broadcasted_iota(jnp.int32, (TOK, LANES), 1), TOK - 1)
    valid_lane = (tlane < t_new).astype(F32)
    dnorm = dnorm_ref[...]
    first = HIST_ROWS - (CONV_W - 1)

    def stack(a, base):
        return jnp.concatenate(
            [a[:, base + h * B_HEAD_DIM:base + (h + 1) * B_HEAD_DIM] for h in range(B_HEADS)], axis=0)

    def rows_of(col_of):
        return jnp.concatenate(
            [jnp.broadcast_to(col_of(h), (TOK, LANES)) for h in range(B_HEADS)], axis=0)

    def seq(b):
        xx_ref[0:HIST_ROWS, :] = hist_ref[b]
        xx_ref[HIST_ROWS:, :] = x_ref[b, :, 0:CONV_DIM]
        y = xx_ref[first:first + TOK, :] * convw_ref[0:1, :]
        for j in range(1, CONV_W):
            y = y + xx_ref[first + j:first + j + TOK, :] * convw_ref[j:j + 1, :]
        y = y * jax.nn.sigmoid(y) * valid_t
        q = _l2(stack(y, 0)) * (B_HEAD_DIM ** -0.5)
        k = _l2(stack(y, QB_DIM))
        v = stack(y, 2 * QB_DIM)

        ab = ab_ref[b]
        beta = jax.nn.sigmoid(ab)
        g = -jnp.exp(arow_ref[0:1, :]) * _softplus(ab + arow_ref[1:2, :]) * valid_t
        gl = (-jnp.exp(alane_ref[0:1, :]) * _softplus(abt_ref[b] + alane_ref[1:2, :])) * valid_lane
        step = 1
        while step < TOK:
            g = g + jnp.where(trow >= step, pltpu.roll(g, step, 0), 0.0)
            gl = gl + jnp.where(tlane >= step, pltpu.roll(gl, step, 1), 0.0)
            step *= 2
        gb = rows_of(lambda h: g[:, B_HEADS + h:B_HEADS + h + 1])
        bt = rows_of(lambda h: beta[:, h:h + 1])
        glast = rows_of(lambda h: g[TOK - 1:TOK, B_HEADS + h:B_HEADS + h + 1])
        u, w, qk, qd, kd = yield from _delta_tile(q, k, v, gb, bt, gl[0:1, :], glast, masks)

        wq = []
        for h in range(B_HEADS):
            rs = slice(h * TOK, (h + 1) * TOK)
            wq.append(_mm(jnp.concatenate([w[rs], qd[rs]], axis=0), s0_ref[b, h]))
        yield
        v_new = u - jnp.concatenate([x[:TOK] for x in wq], axis=0)
        o = jnp.concatenate([x[TOK:] for x in wq], axis=0) + _mm(qk, v_new)
        yield
        for h in range(B_HEADS):
            rs = slice(h * TOK, (h + 1) * TOK)
            sout_ref[b, h] = (s0_ref[b, h] * jnp.exp(glast[h * TOK:h * TOK + 1, :])
                              + _mm_tn(kd[rs], v_new[rs]))
        ob = _gated_norm(o, stack(x_ref[b], COL_ZB), dnorm)
        for h in range(B_HEADS):
            ob_scr[b * t_new:(b + 1) * t_new, h * B_HEAD_DIM:(h + 1) * B_HEAD_DIM] = (
                ob[h * TOK:h * TOK + t_new])

    _interleave(seq(b) for b in range(bb))
    ob_ref[...] = ob_scr[...].astype(ob_ref.dtype)


def _delta_sample(xs, ab, abt, hist, s0, conv_w, arow, alane, dnorm, t_new, o_full, row0):
    bd = xs.shape[0]
    bb = _pick(bd, 4, 1)
    rows = bb * t_new
    assert row0 % rows == 0 and rows % 16 == 0
    blk = lambda a: pl.BlockSpec((bb,) + a.shape[1:], lambda i: (i,) + (0,) * (a.ndim - 1))
    whole = lambda a: pl.BlockSpec(a.shape, lambda i: (0,) * a.ndim)
    return pl.pallas_call(
        functools.partial(_delta_sample_kernel, bb=bb, t_new=t_new),
        grid=(bd // bb,),
        in_specs=[blk(xs), blk(ab), blk(abt), blk(hist), blk(s0),
                  whole(conv_w), whole(arow), whole(alane), whole(dnorm),
                  pl.BlockSpec(memory_space=pl.ANY)],
        out_specs=[pl.BlockSpec((rows, QB_DIM), lambda i: (row0 // rows + i, 0)), blk(s0)],
        out_shape=[jax.ShapeDtypeStruct(o_full.shape, o_full.dtype),
                   jax.ShapeDtypeStruct(s0.shape, F32)],
        scratch_shapes=[pltpu.VMEM((HIST_ROWS + TOK, CONV_DIM), F32),
                        pltpu.VMEM((rows, QB_DIM), F32)],
        input_output_aliases={9: 0},
        compiler_params=_params(("parallel",)),
        name="delta_sample",
    )(xs, ab, abt, hist, s0, conv_w, arow, alane, dnorm, o_full)


def _layer(rows, xp, meta, xs, tables, nb, lp, n_tok, bd, t_new, cache_k, cache_v, state_conv,
           state_delta, norm_mix_pre, norm_mix_post, norm_mlp_pre, norm_mlp_post, w_in, sinks, conv_w,
           a_log, dt_bias, delta_norm, w_branch_a, w_branch_b, w_out, w_up, w_down):
    p_rows = nb * lp

    assert SRC_CONV % MM_TN == 0 and COL_QA % MM_TN == 0 and SRC_BETA % LANES == 0
    n_front, src_front = COL_QA // MM_TN, SRC_CONV // MM_TN
    w_in_t = w_in.T
    hn = _norm_in(rows, xp, meta, xs, norm_mix_pre)
    z = _matmul(hn, w_in_t, BF16, name="in_proj", n=Z_DIM, w_t=True,
                w_col=lambda j: jnp.where(j < n_front, j + src_front, j - n_front))
    ab = _matmul(hn, w_in_t, F32, name="gate_proj", n=LANES, w_t=True,
                 w_col=lambda j: SRC_BETA // LANES)
    zg = _matmul(hn, w_in_t, BF16, name="gate_in_proj", n=2 * D_MODEL, w_t=True, w_n0=SRC_GA)

    qr, kr = _rope(z, tables)
    oa_p = _attn_prompt(qr, kr, z, sinks, nb, lp)

    def sample_pad(a, c0=0, c1=None):
        a = a[p_rows:, c0:c1].astype(F32)
        return jnp.pad(a.reshape(bd, t_new, a.shape[1]), ((0, 0), (0, SAMPLE_ROWS - t_new), (0, 0)))

    n_buf = cache_k.shape[1]
    oa, k_s, v_s = _attn_sample(
        sample_pad(qr), sample_pad(kr), sample_pad(z, COL_VA, COL_VA + KVA_DIM),
        cache_k.reshape(bd, n_buf, KVA_DIM), cache_v.reshape(bd, n_buf, KVA_DIM), sinks, t_new,
        oa_p, p_rows)

    arow = jnp.zeros((2, LANES), F32)
    arow = arow.at[0, B_HEADS:2 * B_HEADS].set(a_log).at[1, B_HEADS:2 * B_HEADS].set(dt_bias)
    acol = jnp.zeros((2 * B_HEADS, LANES), F32)
    acol = acol.at[B_HEADS:, 0].set(a_log).at[B_HEADS:, 1].set(dt_bias)
    alane = jnp.stack([jnp.repeat(a_log, TOK), jnp.repeat(dt_bias, TOK)])
    dnorm = delta_norm.reshape(1, B_HEAD_DIM)
    abt_p = jnp.swapaxes(
        ab[:p_rows].reshape(nb, lp // TILE, TILE, LANES)[..., :2 * B_HEADS], 2, 3)
    ob_p, d_p = _delta_prompt(
        z, ab, abt_p, jnp.zeros((nb, HIST_ROWS, CONV_DIM), F32),
        jnp.zeros((nb, B_HEADS, B_HEAD_DIM, B_HEAD_DIM), F32),
        conv_w, arow, acol, dnorm, nseq=nb, rows_per_seq=lp, front=FRONT, n_valid=n_tok)

    x_tok = sample_pad(z, 0, COL_ZB + QB_DIM)
    ab_s = sample_pad(ab)
    abt_s = jnp.swapaxes(ab_s[..., B_HEADS:2 * B_HEADS], 1, 2).reshape(bd, 1, B_HEADS * TOK)
    hist_s = jnp.pad(state_conv, ((0, 0), (HIST_ROWS - (CONV_W - 1), 0), (0, 0)))
    ob, d_s = _delta_sample(x_tok, ab_s, abt_s, hist_s, state_delta, conv_w, arow, alane, dnorm, t_new,
                            ob_p, p_rows)

    merged = _merge(oa, ob, w_branch_a, w_branch_b, zg)
    y = _matmul(merged, w_out, F32, name="out_proj")
    h1, hn2 = _post_pre(rows, xp, meta, xs, y, norm_mix_post, norm_mlp_pre)
    u = _matmul(hn2, w_up, BF16, relu2=True, name="mlp_up")
    y2 = _matmul(u, w_down, F32, name="mlp_down")
    y_prompt, y_sample = _post_out(rows, h1, y2, norm_mlp_post, xp.shape[1])

    last = FRONT + n_tok

    def tail_rows(a, n, c0, c1):
        return jnp.stack([a[b * lp + last - n:b * lp + last, c0:c1] for b in range(nb)]).astype(F32)

    k_p = tail_rows(kr, WINDOW, 0, KVA_DIM)
    v_p = tail_rows(z, WINDOW, COL_VA, COL_VA + KVA_DIM)
    c_p = tail_rows(z, CONV_W - 1, 0, CONV_DIM)
    x_s = z[p_rows:, :CONV_DIM].astype(F32).reshape(bd, t_new, CONV_DIM)
    c_s = jnp.concatenate([state_conv, x_s], axis=1)[:, -(CONV_W - 1):]
    return y_prompt, y_sample, k_p, v_p, c_p, d_p, k_s, v_s, c_s, d_s


def kernel(x_prompt, x_sample, cache_win_k, cache_win_v, state_conv, state_delta, meta_tokens,
           norm_mix_pre, norm_mix_post, norm_mlp_pre, norm_mlp_post, w_in, sinks, conv_w,
           a_log, dt_bias, delta_norm, w_branch_a, w_branch_b, w_out, w_up, w_down):
    nb, seq, d = x_prompt.shape
    bd, t_new, _ = x_sample.shape
    depth = w_in.shape[0]
    assert depth == 1 and CONV_W - 1 <= t_new <= SAMPLE_ROWS == TOK
    n_tok = N_META + seq
    assert n_tok >= WINDOW
    lp = -(-(FRONT + n_tok) // WINDOW) * WINDOW
    rows = _Rows(nb, seq, lp, bd * t_new)

    pos_p = jnp.maximum(jnp.arange(lp) - FRONT, 0)
    pos = jnp.concatenate([jnp.tile(pos_p, nb), jnp.tile(PAST_LEN + jnp.arange(t_new), bd)])
    tables = _rope_tables(pos)

    l = 0
    y_prompt, y_sample, k_p, v_p, c_p, d_p, k_s, v_s, c_s, d_s = _layer(
        rows, x_prompt, meta_tokens, x_sample.reshape(bd * t_new, d), tables, nb, lp, n_tok, bd,
        t_new, cache_win_k[l], cache_win_v[l], state_conv[l],
        state_delta[l], norm_mix_pre[l], norm_mix_post[l], norm_mlp_pre[l], norm_mlp_post[l],
        w_in[l], sinks[l], conv_w[l], a_log[l], dt_bias[l], delta_norm[l], w_branch_a[l],
        w_branch_b[l], w_out[l], w_up[l], w_down[l])

    y_sample = y_sample.reshape(bd, t_new, d)
    n_buf = cache_win_k.shape[2]
    kv = lambda a, n, r: a.reshape(1, n, r, A_KV_HEADS, A_HEAD_DIM)
    return (y_prompt, y_sample,
            kv(k_p, nb, WINDOW), kv(v_p, nb, WINDOW), c_p[None], d_p[None],
            kv(k_s, bd, n_buf), kv(v_s, bd, n_buf), c_s[None], d_s[None])
```
